```python
import math
import jax
import jax.numpy as jnp
from jax import lax
import numpy as np


D_MODEL = 2048
BATCH = 2
SEQ = 4096
DEPTH = 2
DEC_BATCH = 8
DEC_SEQ = 4
PAST_LEN = 16384
PAGE_SIZE = 128

HEAD_DIM = 128
H_A = 6
H_B = 5
H_C = 5
DA_QK = HEAD_DIM // 2
H_IDX = 16
D_IDX = 64
TOPK_MAX = 256
ROPE_THETA = 500000.0
ROT_FRAC = 4
N_EXPERTS = 32
TOP_K_EXPERTS = 4
D_FF = D_MODEL
SWIGLU_LIMIT = 7.0
SWIGLU_ALPHA = 1.702
Q_BLOCK = 128
LN_EPS = 1e-5
RMS_EPS = 1e-6
DEEPNORM_ALPHA = (2 * DEPTH) ** 0.25
DEEPNORM_BETA = (8 * DEPTH) ** -0.25
IN_SPLITS = (2 * H_A * DA_QK, 2 * H_A * DA_QK, H_A * HEAD_DIM,
             H_B * HEAD_DIM, H_B * HEAD_DIM, H_B * HEAD_DIM,
             H_IDX * D_IDX, H_IDX, D_IDX,
             H_C * HEAD_DIM, H_C * HEAD_DIM, H_C * HEAD_DIM)
N_IN = sum(IN_SPLITS)
V_SEGMENTS = (2, 5, 11)

kernel_name = 'hymba_diff_dsa_stickbreak_moe_step'


def split_points():
    pts, acc = [], 0
    for s in IN_SPLITS[:-1]:
        acc += s
        pts.append(acc)
    return pts


def layer_norm(x, g, b):
    xf = x.astype(jnp.float32)
    mu = jnp.mean(xf, axis=-1, keepdims=True)
    var = jnp.mean(jnp.square(xf - mu), axis=-1, keepdims=True)
    return ((xf - mu) * lax.rsqrt(var + LN_EPS) * g + b).astype(x.dtype)


def partial_rope(x, pos):
    d = x.shape[-1]
    r = d // ROT_FRAC
    half = r // 2
    inv = ROPE_THETA ** (-(jnp.arange(half, dtype=jnp.float32) * 2.0 / r))
    ang = pos.astype(jnp.float32)[:, None] * inv[None, :]
    cos = jnp.cos(ang)[:, None, :]
    sin = jnp.sin(ang)[:, None, :]
    xr = x[..., :r].astype(jnp.float32)
    x1, x2 = xr[..., :half], xr[..., half:]
    rot = jnp.concatenate([x1 * cos - x2 * sin, x2 * cos + x1 * sin], axis=-1).astype(x.dtype)
    return jnp.concatenate([rot, x[..., r:]], axis=-1)


def diff_attention(q, k, v, q_pos, k_pos, lam):
    s = jnp.einsum('bqhcd,bshcd->bhcqs', q, k, preferred_element_type=jnp.float32) * (DA_QK ** -0.5)
    mask = k_pos[None, :] <= q_pos[:, None]
    p = jax.nn.softmax(jnp.where(mask, s, -jnp.inf), axis=-1)
    w = p[:, :, 0] - lam * p[:, :, 1]
    return jnp.einsum('bhqs,bshd->bqhd', w.astype(v.dtype), v)


def indexed_sparse_attention(q, qi, wi, k, v, ki, q_pos, k_pos, topk):
    dots = jnp.einsum('bqhd,bsd->bqhs', qi, ki, preferred_element_type=jnp.float32) * (D_IDX ** -0.5)
    score = jnp.einsum('bqh,bqhs->bqs', wi.astype(jnp.float32), jax.nn.relu(dots))
    visible = k_pos[None, :] <= q_pos[:, None]
    score = jnp.where(visible[None], score, -jnp.inf)
    _, idx = lax.top_k(score, topk)
    sel_ok = jnp.take(k_pos, idx) <= q_pos[None, :, None]
    bidx = jnp.arange(k.shape[0])[:, None, None]
    k_sel = k[bidx, idx]
    v_sel = v[bidx, idx]
    s = jnp.einsum('bqhd,bqkhd->bhqk', q, k_sel, preferred_element_type=jnp.float32) * (HEAD_DIM ** -0.5)
    p = jax.nn.softmax(jnp.where(sel_ok[:, None], s, -jnp.inf), axis=-1)
    return jnp.einsum('bhqk,bqkhd->bqhd', p.astype(v.dtype), v_sel)


def stick_breaking_attention(q, k, v, q_pos, k_pos):
    z = jnp.einsum('bqhd,bshd->bhqs', q, k, preferred_element_type=jnp.float32) * (HEAD_DIM ** -0.5)
    before = k_pos[None, :] < q_pos[:, None]
    log_1m = jnp.where(before, jax.nn.log_sigmoid(-z), 0.0)
    tail = lax.cumsum(log_1m, axis=3, reverse=True) - log_1m
    log_a = jnp.where(before, jax.nn.log_sigmoid(z) + tail, -jnp.inf)
    return jnp.einsum('bhqs,bshd->bqhd', jnp.exp(log_a).astype(v.dtype), v)


def over_query_blocks(fn, q_args, q_pos):
    b, t = q_args[0].shape[:2]
    nb = t // Q_BLOCK

    def split(a):
        return jnp.swapaxes(a.reshape(b, nb, Q_BLOCK, *a.shape[2:]), 0, 1)

    blocks = tuple(split(a) for a in q_args)
    out = lax.map(lambda xs: fn(*xs[0], xs[1]), (blocks, q_pos.reshape(nb, Q_BLOCK)))
    return jnp.swapaxes(out, 0, 1).reshape(b, t, *out.shape[3:])


def project_heads(h, pos, w_in, ln_idx_g, ln_idx_b):
    b, t, _ = h.shape
    proj = jnp.einsum('btd,dn->btn', h, w_in)
    qa, ka, va, qb, kb, vb, qi, wi, ki, qc, kc, vc = jnp.split(proj, split_points(), axis=-1)
    qa = partial_rope(qa.reshape(b, t, 2 * H_A, DA_QK), pos).reshape(b, t, H_A, 2, DA_QK)
    ka = partial_rope(ka.reshape(b, t, 2 * H_A, DA_QK), pos).reshape(b, t, H_A, HEAD_DIM)
    qb = partial_rope(qb.reshape(b, t, H_B, HEAD_DIM), pos)
    kb = partial_rope(kb.reshape(b, t, H_B, HEAD_DIM), pos)
    qi = partial_rope(qi.reshape(b, t, H_IDX, D_IDX), pos)
    wi = wi * (H_IDX ** -0.5)
    ki = partial_rope(layer_norm(ki, ln_idx_g, ln_idx_b)[:, :, None, :], pos)[:, :, 0, :]
    kv_a = jnp.stack([ka, va.reshape(b, t, H_A, HEAD_DIM)], axis=2)
    kv_b = jnp.stack([kb, vb.reshape(b, t, H_B, HEAD_DIM)], axis=2)
    kv_c = jnp.stack([kc.reshape(b, t, H_C, HEAD_DIM), vc.reshape(b, t, H_C, HEAD_DIM)], axis=2)
    queries = (qa, qb, qi, wi, qc.reshape(b, t, H_C, HEAD_DIM))
    return queries, (kv_a, kv_b, ki, kv_c)


def token_mix(queries, rows, q_pos, k_pos, topk, lam):
    qa, qb, qi, wi, qc = queries
    kv_a, kv_b, ki, kv_c = rows
    b, s = kv_a.shape[:2]
    ka = kv_a[:, :, 0].reshape(b, s, H_A, 2, DA_QK)
    va = kv_a[:, :, 1]
    kb, vb = kv_b[:, :, 0], kv_b[:, :, 1]
    kc, vc = kv_c[:, :, 0], kv_c[:, :, 1]
    fa = lambda q, p: diff_attention(q, ka, va, p, k_pos, lam)
    fb = lambda q, qi_, wi_, p: indexed_sparse_attention(q, qi_, wi_, kb, vb, ki, p, k_pos, topk)
    fc = lambda q, p: stick_breaking_attention(q, kc, vc, p, k_pos)
    if qa.shape[1] % Q_BLOCK == 0:
        oa = over_query_blocks(fa, (qa,), q_pos)
        ob = over_query_blocks(fb, (qb, qi, wi), q_pos)
        oc = over_query_blocks(fc, (qc,), q_pos)
    else:
        oa = fa(qa, q_pos)
        ob = fb(qb, qi, wi, q_pos)
        oc = fc(qc, q_pos)
    return oa, ob, oc


def mixer_sublayer(h, q_pos, k_pos, past, topk, layer_idx, w_in, ln_idx_g, ln_idx_b, lam_p, gn_gain, w_out):
    queries, new_rows = project_heads(h, q_pos, w_in, ln_idx_g, ln_idx_b)
    if past is None:
        rows = new_rows
    else:
        rows = tuple(jnp.concatenate([p, n.astype(p.dtype)], axis=1) for p, n in zip(past, new_rows))
    lam_init = 0.8 - 0.6 * math.exp(-0.3 * layer_idx)
    lp = lam_p.astype(jnp.float32)
    lam = jnp.exp(jnp.sum(lp[0] * lp[1])) - jnp.exp(jnp.sum(lp[2] * lp[3])) + lam_init
    oa, ob, oc = token_mix(queries, rows, q_pos, k_pos, topk, lam)
    oa32 = oa.astype(jnp.float32)
    oa = (oa32 * lax.rsqrt(jnp.mean(jnp.square(oa32), axis=-1, keepdims=True) + RMS_EPS)
          * gn_gain * (1.0 - lam_init)).astype(h.dtype)
    b, t = h.shape[:2]
    o = jnp.concatenate([oa.reshape(b, t, -1), ob.reshape(b, t, -1), oc.reshape(b, t, -1)], axis=-1)
    return jnp.einsum('btd,de->bte', o, w_out), new_rows


def moe_ffn(h, w_router, b_router, w_gate_up, b_gate_up, w_down, b_down):
    b, t, d = h.shape
    tok = h.reshape(b * t, d)
    logits = (tok @ w_router + b_router).astype(jnp.float32)
    top_val, top_idx = lax.top_k(logits, TOP_K_EXPERTS)
    top_w = jax.nn.softmax(top_val, axis=-1)
    gate = jnp.sum(jax.nn.one_hot(top_idx, N_EXPERTS, dtype=jnp.float32) * top_w[..., None], axis=1).astype(h.dtype)
    out = jnp.zeros_like(tok)
    for e in range(N_EXPERTS):
        gu = tok @ w_gate_up[e] + b_gate_up[e]
        g = jnp.minimum(gu[:, :D_FF], SWIGLU_LIMIT)
        u = jnp.clip(gu[:, D_FF:], -SWIGLU_LIMIT, SWIGLU_LIMIT)
        act = (u + 1.0) * g * jax.nn.sigmoid(SWIGLU_ALPHA * g)
        out = out + gate[:, e:e + 1] * (act @ w_down[e] + b_down[e])
    return out.reshape(b, t, d)


def decoder_layer(x, c, q_pos, k_pos, past, topk, layer_idx, w_ada, b_ada, w_in, ln_idx_g, ln_idx_b,
                  lam_p, gn_gain, w_out, ln1_g, ln1_b, w_router, b_router, w_gate_up, b_gate_up,
                  w_down, b_down, ln2_g, ln2_b):
    mod = (jnp.einsum('bd,dn->bn', jax.nn.silu(c), w_ada) + b_ada).reshape(c.shape[0], 6, 1, D_MODEL)
    sh_a, sc_a, g_a, sh_f, sc_f, g_f = [mod[:, i] for i in range(6)]
    h = x * (1.0 + sc_a) + sh_a
    attn, new_rows = mixer_sublayer(h, q_pos, k_pos, past, topk, layer_idx, w_in, ln_idx_g, ln_idx_b,
                                    lam_p, gn_gain, w_out)
    x = layer_norm(DEEPNORM_ALPHA * x + (1.0 + g_a) * attn, ln1_g, ln1_b)
    h = x * (1.0 + sc_f) + sh_f
    ffn = moe_ffn(h, w_router, b_router, w_gate_up, b_gate_up, w_down, b_down)
    x = layer_norm(DEEPNORM_ALPHA * x + (1.0 + g_f) * ffn, ln2_g, ln2_b)
    return x, new_rows


def gather_pages(pool, layer_idx, page_table):
    g = pool[layer_idx, page_table]
    return g.reshape(g.shape[0], g.shape[1] * g.shape[2], *g.shape[3:])


def setup_inputs(seed: int = 0) -> dict:
    key = jax.random.key(seed)
    ks = iter(jax.random.split(key, 40))
    f32 = jnp.float32
    n_pages = PAST_LEN // PAGE_SIZE
    n_used = DEC_BATCH * n_pages
    n_pool = n_used + max(1, n_used // 4)

    def nrm(shape, scale):
        return jax.random.normal(next(ks), shape, f32) * scale

    perm = jax.random.permutation(next(ks), n_pool)
    page_table = perm[:n_used].reshape(DEC_BATCH, n_pages).astype(jnp.int32)
    col_scale = np.ones((N_IN,), np.float32)
    pts = [0] + split_points() + [N_IN]
    for i in V_SEGMENTS:
        col_scale[pts[i]:pts[i + 1]] = DEEPNORM_BETA
    return {
        'x_prompt': nrm((BATCH, SEQ, D_MODEL), 1.0),
        'x_sample': nrm((DEC_BATCH, DEC_SEQ, D_MODEL), 1.0),
        'cache_kv_a': nrm((DEPTH, n_pool, PAGE_SIZE, 2, H_A, HEAD_DIM), 1.0),
        'cache_kv_b': nrm((DEPTH, n_pool, PAGE_SIZE, 2, H_B, HEAD_DIM), 1.0),
        'cache_idx_b': nrm((DEPTH, n_pool, PAGE_SIZE, D_IDX), 1.0),
        'cache_kv_c': nrm((DEPTH, n_pool, PAGE_SIZE, 2, H_C, HEAD_DIM), 1.0),
        'page_table': page_table,
        'c_prompt': nrm((BATCH, D_MODEL), 1.0),
        'c_sample': nrm((DEC_BATCH, D_MODEL), 1.0),
        'w_ada': nrm((DEPTH, D_MODEL, 6 * D_MODEL), 0.2 * D_MODEL ** -0.5),
        'b_ada': nrm((DEPTH, 6 * D_MODEL), 0.01),
        'w_in': nrm((DEPTH, D_MODEL, N_IN), D_MODEL ** -0.5) * jnp.asarray(col_scale),
        'ln_idx_g': 1.0 + nrm((DEPTH, D_IDX), 0.02),
        'ln_idx_b': nrm((DEPTH, D_IDX), 0.02),
        'lam_params': nrm((DEPTH, 4, DA_QK), 0.1),
        'gn_a_gain': 1.0 + nrm((DEPTH, H_A, HEAD_DIM), 0.02),
        'w_out': nrm((DEPTH, D_MODEL, D_MODEL), DEEPNORM_BETA * D_MODEL ** -0.5),
        'ln1_g': 1.0 + nrm((DEPTH, D_MODEL), 0.02),
        'ln1_b': nrm((DEPTH, D_MODEL), 0.02),
        'w_router': nrm((DEPTH, D_MODEL, N_EXPERTS), D_MODEL ** -0.5),
        'b_router': nrm((DEPTH, N_EXPERTS), 0.01),
        'w_gate_up': nrm((DEPTH, N_EXPERTS, D_MODEL, 2 * D_FF), D_MODEL ** -0.5),
        'b_gate_up': nrm((DEPTH, N_EXPERTS, 2 * D_FF), 0.01),
        'w_down': nrm((DEPTH, N_EXPERTS, D_FF, D_MODEL), DEEPNORM_BETA * D_FF ** -0.5),
        'b_down': nrm((DEPTH, N_EXPERTS, D_MODEL), 0.01),
        'ln2_g': 1.0 + nrm((DEPTH, D_MODEL), 0.02),
        'ln2_b': nrm((DEPTH, D_MODEL), 0.02),
    }


def reference(x_prompt, x_sample, cache_kv_a, cache_kv_b, cache_idx_b, cache_kv_c, page_table,
              c_prompt, c_sample, w_ada, b_ada, w_in, ln_idx_g, ln_idx_b, lam_params, gn_a_gain,
              w_out, ln1_g, ln1_b, w_router, b_router, w_gate_up, b_gate_up, w_down, b_down,
              ln2_g, ln2_b):
    t_p = x_prompt.shape[1]
    t_s = x_sample.shape[1]
    past_len = page_table.shape[1] * cache_kv_a.shape[2]
    pos_p = jnp.arange(t_p, dtype=jnp.int32)
    pos_s = past_len + jnp.arange(t_s, dtype=jnp.int32)
    kpos_s = jnp.arange(past_len + t_s, dtype=jnp.int32)
    topk_p = min(TOPK_MAX, t_p // 4)
    topk_s = min(TOPK_MAX, (past_len + t_s) // 4)
    xp, xs = x_prompt, x_sample
    rows_p, rows_s = [], []
    for l in range(DEPTH):
        weights = (w_ada[l], b_ada[l], w_in[l], ln_idx_g[l], ln_idx_b[l], lam_params[l], gn_a_gain[l],
                   w_out[l], ln1_g[l], ln1_b[l], w_router[l], b_router[l], w_gate_up[l], b_gate_up[l],
                   w_down[l], b_down[l], ln2_g[l], ln2_b[l])
        xp, new_p = decoder_layer(xp, c_prompt, pos_p, pos_p, None, topk_p, l, *weights)
        past = (gather_pages(cache_kv_a, l, page_table), gather_pages(cache_kv_b, l, page_table),
                gather_pages(cache_idx_b, l, page_table), gather_pages(cache_kv_c, l, page_table))
        xs, new_s = decoder_layer(xs, c_sample, pos_s, kpos_s, past, topk_s, l, *weights)
        rows_p.append(new_p)
        rows_s.append(new_s)
    new_kv_a_prompt = jnp.stack([r[0] for r in rows_p])
    new_kv_b_prompt = jnp.stack([r[1] for r in rows_p])
    new_idx_b_prompt = jnp.stack([r[2] for r in rows_p])
    new_kv_c_prompt = jnp.stack([r[3] for r in rows_p])
    new_kv_a_sample = jnp.stack([r[0] for r in rows_s])
    new_kv_b_sample = jnp.stack([r[1] for r in rows_s])
    new_idx_b_sample = jnp.stack([r[2] for r in rows_s])
    new_kv_c_sample = jnp.stack([r[3] for r in rows_s])
    return (xp, xs, new_kv_a_prompt, new_kv_b_prompt, new_idx_b_prompt, new_kv_c_prompt,
            new_kv_a_sample, new_kv_b_sample, new_idx_b_sample, new_kv_c_sample)
```

```python
import functools
import math

import jax
import jax.numpy as jnp
from jax import lax
from jax.experimental import pallas as pl
from jax.experimental.pallas import tpu as pltpu

BF = jnp.bfloat16
F32 = jnp.float32
I32 = jnp.int32

HEAD_DIM = 128
H_A, H_B, H_C = 6, 5, 5
DA_QK = 64
H_IDX, D_IDX = 16, 64
TOPK_MAX = 256
ROPE_THETA = 500000.0
ROT_FRAC = 4
TOP_K_EXPERTS = 4
SWIGLU_LIMIT = 7.0
SWIGLU_ALPHA = 1.702
LN_EPS = 1e-5
RMS_EPS = 1e-6
LANE = 128
NEG = -1e30
KEY_NEG_INF = -2139095041
VMEM_LIMIT = 56 * 1024 * 1024

W_A = H_A * HEAD_DIM
W_B = H_B * HEAD_DIM
W_C = H_C * HEAD_DIM
W_QI = H_IDX * D_IDX
S_QA, S_KA, S_VA = 0, W_A, 2 * W_A
S_QB = 3 * W_A
S_KB, S_VB = S_QB + W_B, S_QB + 2 * W_B
S_QI = S_QB + 3 * W_B
S_WI = S_QI + W_QI
S_KI = S_WI + H_IDX
S_QC = S_KI + D_IDX
S_KC, S_VC = S_QC + W_C, S_QC + 2 * W_C
N_IN = S_QC + 3 * W_C
P_QA, P_KA, P_VA, P_QB, P_KB, P_VB, P_QI = S_QA, S_KA, S_VA, S_QB, S_KB, S_VB, S_QI
P_WI = P_QI + W_QI
P_KI = P_WI + LANE
P_QC = P_KI + LANE
P_KC, P_VC = P_QC + W_C, P_QC + 2 * W_C
N_PAD = P_QC + 3 * W_C


def _cparams(sem, vmem=VMEM_LIMIT):
    return pltpu.CompilerParams(dimension_semantics=sem, vmem_limit_bytes=vmem)


def _dot(a, b):
    return jnp.dot(a, b, preferred_element_type=F32)


def _dot_nt(a, b):
    return lax.dot_general(a, b, (((1,), (1,)), ((), ())), preferred_element_type=F32)


def _iota(shape, dim):
    return lax.broadcasted_iota(I32, shape, dim)


def _ada_kernel(c_ref, w_ref, b_ref, o_ref):
    c = c_ref[...]
    s = c * jax.nn.sigmoid(c)
    o_ref[...] = _dot(s, w_ref[...]) + b_ref[...]


def ada_modulation(c_all, w_ada, b_ada, tn=1024):
    depth, d, n = w_ada.shape
    rows = c_all.shape[0]
    return pl.pallas_call(
        _ada_kernel,
        grid=(depth, n // tn),
        in_specs=[
            pl.BlockSpec((rows, d), lambda l, j: (0, 0)),
            pl.BlockSpec((None, d, tn), lambda l, j: (l, 0, j)),
            pl.BlockSpec((None, 1, tn), lambda l, j: (l, 0, j)),
        ],
        out_specs=pl.BlockSpec((None, rows, tn), lambda l, j: (l, 0, j)),
        out_shape=jax.ShapeDtypeStruct((depth, rows, n), F32),
        compiler_params=_cparams(("parallel", "parallel")),
        name="ada_modulation",
    )(c_all, w_ada, b_ada.reshape(depth, 1, n))


def _proj_kernel(x_ref, sc_ref, sh_ref, w_ref, o_ref, h_sc):
    @pl.when(pl.program_id(2) == 0)
    def _():
        h_sc[...] = (x_ref[...] * (1.0 + sc_ref[...]) + sh_ref[...]).astype(BF)

    o_ref[...] = _dot(h_sc[...], w_ref[...])


def modulated_projection(x, sc, sh, w, layer, tm, tn=256):
    b, t, d = x.shape
    n = w.shape[-1]
    ts = sc.shape[1]
    tms = 1 if ts == 1 else tm
    mod_map = (lambda bi, i, j: (bi, 0, 0)) if ts == 1 else (lambda bi, i, j: (bi, i, 0))
    return pl.pallas_call(
        _proj_kernel,
        grid=(b, t // tm, n // tn),
        in_specs=[
            pl.BlockSpec((None, tm, d), lambda bi, i, j: (bi, i, 0)),
            pl.BlockSpec((None, tms, d), mod_map),
            pl.BlockSpec((None, tms, d), mod_map),
            pl.BlockSpec((None, d, tn), lambda bi, i, j: (layer, 0, j)),
        ],
        out_specs=pl.BlockSpec((None, tm, tn), lambda bi, i, j: (bi, i, j)),
        out_shape=jax.ShapeDtypeStruct((b, t, n), F32),
        scratch_shapes=[pltpu.VMEM((tm, d), BF)],
        compiler_params=_cparams(("parallel", "parallel", "arbitrary")),
        name="modulated_projection",
    )(x, sc, sh, w)


def _rope(x, tab_ref, half):
    c = tab_ref[:, 0:LANE]
    sa = tab_ref[:, LANE:2 * LANE]
    sb = tab_ref[:, 2 * LANE:3 * LANE]
    outs = []
    for blk in range(x.shape[1] // LANE):
        xb = x[:, blk * LANE:(blk + 1) * LANE]
        outs.append(xb * c + pltpu.roll(xb, half, 1) * sa + pltpu.roll(xb, LANE - half, 1) * sb)
    return outs[0] if len(outs) == 1 else jnp.concatenate(outs, axis=1)


def _epilogue_kernel(p_ref, t64_ref, t128_ref, ln_ref,
                     kva_ref, kvb_ref, idx_ref, kvc_ref,
                     qa_ref, ka_ref, va_ref, qb_ref, kb_ref, vb_ref,
                     qi_ref, wi_ref, ki2_ref, qc_ref, kc_ref, vc_ref):
    h64 = DA_QK // ROT_FRAC // 2
    h128 = HEAD_DIM // ROT_FRAC // 2
    qa = _rope(p_ref[:, P_QA:P_QA + W_A], t64_ref, h64)
    ka = _rope(p_ref[:, P_KA:P_KA + W_A], t64_ref, h64)
    va = p_ref[:, P_VA:P_VA + W_A]
    qa_ref[...] = qa.astype(BF)
    ka_ref[...] = ka.astype(BF)
    va_ref[...] = va.astype(BF)
    kva_ref[:, 0:W_A] = ka
    kva_ref[:, W_A:2 * W_A] = va

    qb = _rope(p_ref[:, P_QB:P_QB + W_B], t128_ref, h128)
    kb = _rope(p_ref[:, P_KB:P_KB + W_B], t128_ref, h128)
    vb = p_ref[:, P_VB:P_VB + W_B]
    qb_ref[...] = qb.astype(BF)
    kb_ref[...] = kb.astype(BF)
    vb_ref[...] = vb.astype(BF)
    kvb_ref[:, 0:W_B] = kb
    kvb_ref[:, W_B:2 * W_B] = vb

    qi_ref[...] = _rope(p_ref[:, P_QI:P_QI + W_QI], t64_ref, h64).astype(BF)
    wi_ref[...] = p_ref[:, P_WI:P_WI + LANE] * (H_IDX ** -0.5)

    ki = p_ref[:, P_KI:P_KI + LANE]
    real = _iota((1, LANE), 1) < D_IDX
    mu = jnp.sum(ki, axis=1, keepdims=True) * (1.0 / D_IDX)
    cen = jnp.where(real, ki - mu, 0.0)
    var = jnp.sum(cen * cen, axis=1, keepdims=True) * (1.0 / D_IDX)
    kin = cen * lax.rsqrt(var + LN_EPS) * ln_ref[0:1, :] + ln_ref[1:2, :]
    kir = jnp.where(real, _rope(kin, t64_ref, h64), 0.0)
    idx_ref[...] = kir[:, 0:D_IDX]
    kib = kir.astype(BF)
    ki2_ref[:, 0:LANE] = kib
    ki2_ref[:, LANE:2 * LANE] = pltpu.roll(kir, D_IDX, 1).astype(BF)

    kc = p_ref[:, P_KC:P_KC + W_C]
    vc = p_ref[:, P_VC:P_VC + W_C]
    qc_ref[...] = p_ref[:, P_QC:P_QC + W_C].astype(BF)
    kc_ref[...] = kc.astype(BF)
    vc_ref[...] = vc.astype(BF)
    kvc_ref[:, 0:W_C] = kc
    kvc_ref[:, W_C:2 * W_C] = vc


def projection_epilogue(proj, tab64, tab128, ln_gb, tm):
    m = proj.shape[0]
    row = lambda w: pl.BlockSpec((tm, w), lambda i: (i, 0))
    shapes = [
        ((m, 2 * W_A), F32), ((m, 2 * W_B), F32), ((m, D_IDX), F32), ((m, 2 * W_C), F32),
        ((m, W_A), BF), ((m, W_A), BF), ((m, W_A), BF),
        ((m, W_B), BF), ((m, W_B), BF), ((m, W_B), BF),
        ((m, W_QI), BF), ((m, LANE), F32), ((m, 2 * LANE), BF),
        ((m, W_C), BF), ((m, W_C), BF), ((m, W_C), BF),
    ]
    return pl.pallas_call(
        _epilogue_kernel,
        grid=(m // tm,),
        in_specs=[row(N_PAD), row(3 * LANE), row(3 * LANE), pl.BlockSpec((2, LANE), lambda i: (0, 0))],
        out_specs=[row(s[0][1]) for s in shapes],
        out_shape=[jax.ShapeDtypeStruct(*s) for s in shapes],
        compiler_params=_cparams(("parallel",)),
        name="projection_epilogue",
    )(proj, tab64, tab128, ln_gb)


def rope_tables(pos, head_dim):
    r = head_dim // ROT_FRAC
    half = r // 2
    inv = ROPE_THETA ** (-(jnp.arange(half, dtype=F32) * 2.0 / r))
    ang = pos.astype(F32)[:, None] * inv[None, :]
    cos, sin = jnp.cos(ang), jnp.sin(ang)
    n = pos.shape[0]
    pad = jnp.zeros((n, head_dim - r), F32)
    zero = jnp.zeros((n, half), F32)
    c = jnp.concatenate([cos, cos, pad + 1.0], axis=1)
    sa = jnp.concatenate([zero, sin, pad], axis=1)
    sb = jnp.concatenate([-sin, zero, pad], axis=1)
    rep = LANE // head_dim
    return jnp.concatenate([jnp.tile(c, (1, rep)), jnp.tile(sa, (1, rep)), jnp.tile(sb, (1, rep))], axis=1)


def _lambda(lam_ref, lam_init):
    lp = lam_ref[...]
    a = jnp.sum(lp[0:1, :] * lp[1:2, :], axis=1, keepdims=True)
    b = jnp.sum(lp[2:3, :] * lp[3:4, :], axis=1, keepdims=True)
    return jnp.exp(a) - jnp.exp(b) + lam_init


def _flash_update(s, v, m, l, acc):
    mn = jnp.maximum(m, jnp.max(s, axis=1, keepdims=True))
    p = jnp.exp(s - mn)
    al = jnp.exp(m - mn)
    l = al * l + jnp.sum(p, axis=1, keepdims=True)
    acc = al * acc + _dot(p.astype(BF), v)
    return mn, l, acc


def _softplus(z):
    return jnp.maximum(z, 0.0) + jnp.log1p(jnp.exp(-jnp.abs(z)))


def _suffix_matrix(n):
    return jnp.where(_iota((n, n), 0) > _iota((n, n), 1), 1.0, 0.0).astype(BF)


def _prefix_matrix(n):
    return jnp.where(_iota((n, n), 0) <= _iota((n, n), 1), 1.0, 0.0).astype(BF)


def _suffix_sum(x, mat):
    hi = x.astype(BF)
    lo = (x - hi.astype(F32)).astype(BF)
    return _dot(hi, mat) + _dot(lo, mat)


def _sort_key(score):
    bits = pltpu.bitcast(score + 0.0, I32)
    return jnp.where(bits < 0, bits ^ jnp.int32(0x7FFFFFFF), bits)


def _kth_largest_key(count_ge, shape, k):
    t = jnp.full(shape, jnp.iinfo(jnp.int32).min, I32)
    for bit in range(31, -1, -1):
        step = jnp.int32(-2 ** 31) if bit == 31 else jnp.int32(1 << bit)
        cand = t + step
        t = jnp.where(count_ge(cand) >= float(k), cand, t)
    return t


def _diff_finish(acc, l, lam, gain, lam_init, rows_per_map):
    full = acc / l
    other = pltpu.roll(full, rows_per_map, 0)
    o = full - lam * other
    o = o * lax.rsqrt(jnp.mean(o * o, axis=1, keepdims=True) + RMS_EPS)
    return o * gain * (1.0 - lam_init)


def _attn_a_kernel(lam_ref, gain_ref, q_ref, k_ref, v_ref, o_ref, *, tq, tk, lam_init):
    i = pl.program_id(1)
    lam = _lambda(lam_ref, lam_init)
    nch = ((i + 1) * tq) // tk
    qpos = i * tq + _iota((tq, 1), 0)
    first = _iota((1, HEAD_DIM), 1) < DA_QK
    scale = DA_QK ** -0.5
    for h in range(H_A):
        sl = slice(h * HEAD_DIM, (h + 1) * HEAD_DIM)
        q = q_ref[:, sl]
        zero = jnp.zeros_like(q)
        q0 = jnp.where(first, q, zero)
        q1 = jnp.where(first, zero, q)

        def body(c, carry, sl=sl, q0=q0, q1=q1):
            m0, l0, a0, m1, l1, a1 = carry
            off = pl.multiple_of(c * tk, tk)
            k = k_ref[pl.ds(off, tk), sl]
            v = v_ref[pl.ds(off, tk), sl]
            mask = (off + _iota((1, tk), 1)) <= qpos
            s0 = jnp.where(mask, _dot_nt(q0, k) * scale, NEG)
            s1 = jnp.where(mask, _dot_nt(q1, k) * scale, NEG)
            m0, l0, a0 = _flash_update(s0, v, m0, l0, a0)
            m1, l1, a1 = _flash_update(s1, v, m1, l1, a1)
            return m0, l0, a0, m1, l1, a1

        mi = jnp.full((tq, 1), NEG, F32)
        li = jnp.zeros((tq, 1), F32)
        ai = jnp.zeros((tq, HEAD_DIM), F32)
        m0, l0, a0, m1, l1, a1 = lax.fori_loop(0, nch, body, (mi, li, ai, mi, li, ai))
        o = a0 / l0 - lam * (a1 / l1)
        o = o * lax.rsqrt(jnp.mean(o * o, axis=1, keepdims=True) + RMS_EPS)
        o_ref[:, sl] = (o * gain_ref[:, sl] * (1.0 - lam_init)).astype(BF)


def prompt_attention_a(qa, ka, va, lam_params, gain, layer, lam_init, tq, tk):
    b, t, _ = qa.shape
    kern = functools.partial(_attn_a_kernel, tq=tq, tk=tk, lam_init=lam_init)
    return pl.pallas_call(
        kern,
        grid=(b, t // tq),
        in_specs=[
            pl.BlockSpec((None, 4, DA_QK), lambda bi, i: (layer, 0, 0)),
            pl.BlockSpec((None, 1, W_A), lambda bi, i: (layer, 0, 0)),
            pl.BlockSpec((None, tq, W_A), lambda bi, i: (bi, i, 0)),
            pl.BlockSpec((None, t, W_A), lambda bi, i: (bi, 0, 0)),
            pl.BlockSpec((None, t, W_A), lambda bi, i: (bi, 0, 0)),
        ],
        out_specs=pl.BlockSpec((None, tq, W_A), lambda bi, i: (bi, i, 0)),
        out_shape=jax.ShapeDtypeStruct((b, t, W_A), BF),
        compiler_params=_cparams(("parallel", "arbitrary")),
        name="prompt_attention_a",
    )(lam_params, gain, qa, ka, va)


def _stick_chunk(q, k, v, before, mat, run, acc, scale):
    z = _dot_nt(q, k) * scale
    sp = _softplus(z)
    lm = jnp.where(before, -sp, 0.0)
    tail = _suffix_sum(lm, mat)
    log_a = jnp.where(before, z - sp + tail + run, NEG)
    acc = acc + _dot(jnp.exp(log_a).astype(BF), v)
    run = run + jnp.sum(lm, axis=1, keepdims=True)
    return run, acc


def _attn_c_kernel(q_ref, k_ref, v_ref, o_ref, *, tq, tk):
    i = pl.program_id(1)
    nch = ((i + 1) * tq) // tk
    qpos = i * tq + _iota((tq, 1), 0)
    mat = _suffix_matrix(tk)
    scale = HEAD_DIM ** -0.5
    for h in range(H_C):
        sl = slice(h * HEAD_DIM, (h + 1) * HEAD_DIM)
        q = q_ref[:, sl]

        def body(it, carry, sl=sl, q=q):
            run, acc = carry
            off = pl.multiple_of((nch - 1 - it) * tk, tk)
            k = k_ref[pl.ds(off, tk), sl]
            v = v_ref[pl.ds(off, tk), sl]
            before = (off + _iota((1, tk), 1)) < qpos
            return _stick_chunk(q, k, v, before, mat, run, acc, scale)

        init = (jnp.zeros((tq, 1), F32), jnp.zeros((tq, HEAD_DIM), F32))
        _, acc = lax.fori_loop(0, nch, body, init)
        o_ref[:, sl] = acc.astype(BF)


def prompt_attention_c(qc, kc, vc, tq, tk):
    b, t, _ = qc.shape
    kern = functools.partial(_attn_c_kernel, tq=tq, tk=tk)
    return pl.pallas_call(
        kern,
        grid=(b, t // tq),
        in_specs=[
            pl.BlockSpec((None, tq, W_C), lambda bi, i: (bi, i, 0)),
            pl.BlockSpec((None, t, W_C), lambda bi, i: (bi, 0, 0)),
            pl.BlockSpec((None, t, W_C), lambda bi, i: (bi, 0, 0)),
        ],
        out_specs=pl.BlockSpec((None, tq, W_C), lambda bi, i: (bi, i, 0)),
        out_shape=jax.ShapeDtypeStruct((b, t, W_C), BF),
        compiler_params=_cparams(("parallel", "arbitrary")),
        name="prompt_attention_c",
    )(qc, kc, vc)


def _indexer_scores(qi_ref, wi, ki2):
    k_even = ki2[:, 0:LANE]
    k_odd = ki2[:, LANE:2 * LANE]
    acc = None
    for hb in range(H_IDX // 2):
        qblk = qi_ref[:, hb * LANE:(hb + 1) * LANE]
        for par, kk in ((0, k_even), (1, k_odd)):
            h = 2 * hb + par
            d = jnp.maximum(_dot_nt(qblk, kk) * (D_IDX ** -0.5), 0.0) * wi[:, h:h + 1]
            acc = d if acc is None else acc + d
    return acc


def _select_mask(key, t, need, carry, pmat):
    gt = key > t
    eq = key == t
    eqf = jnp.where(eq, 1.0, 0.0)
    rank = _dot(eqf.astype(BF), pmat) + carry
    take = jnp.where(gt, 1.0, jnp.where(rank <= need, eqf, 0.0))
    take = jnp.where(key > KEY_NEG_INF, take, 0.0)
    return jnp.where(take > 0.5, 0.0, NEG), carry + jnp.sum(eqf, axis=1, keepdims=True)


def _attn_b_kernel(q_ref, qi_ref, wi_ref, k_ref, v_ref, ki2_ref, o_ref, key_sc, sel_sc, *, tq, topk):
    tk = tq
    i = pl.program_id(1)
    nch = i + 1
    qpos = i * tq + _iota((tq, 1), 0)
    wi = wi_ref[...]

    def score_body(c, _):
        off = pl.multiple_of(c * tk, tk)
        sc = _indexer_scores(qi_ref, wi, ki2_ref[pl.ds(off, tk), :])
        vis = (off + _iota((1, tk), 1)) <= qpos
        key_sc[c] = _sort_key(jnp.where(vis, sc, -jnp.inf))
        return 0

    lax.fori_loop(0, nch, score_body, 0)

    def count_ge(cand):
        cnt = lax.fori_loop(0, nch, lambda c, a: a + jnp.where(key_sc[c] >= cand, 1.0, 0.0),
                            jnp.zeros((tq, tk), F32))
        return jnp.sum(cnt, axis=1, keepdims=True)

    t = _kth_largest_key(count_ge, (tq, 1), topk)
    cnt_gt = lax.fori_loop(0, nch, lambda c, a: a + jnp.where(key_sc[c] > t, 1.0, 0.0),
                           jnp.zeros((tq, tk), F32))
    need = float(topk) - jnp.sum(cnt_gt, axis=1, keepdims=True)
    pmat = _prefix_matrix(tk)

    def sel_body(c, carry):
        sel_sc[c], carry = _select_mask(key_sc[c], t, need, carry, pmat)
        return carry

    lax.fori_loop(0, nch, sel_body, jnp.zeros((tq, 1), F32))

    scale = HEAD_DIM ** -0.5
    for h in range(H_B):
        sl = slice(h * HEAD_DIM, (h + 1) * HEAD_DIM)
        q = q_ref[:, sl]

        def body(c, carry, sl=sl, q=q):
            m, l, acc = carry
            off = pl.multiple_of(c * tk, tk)
            s = _dot_nt(q, k_ref[pl.ds(off, tk), sl]) * scale + sel_sc[c]
            return _flash_update(s, v_ref[pl.ds(off, tk), sl], m, l, acc)

        init = (jnp.full((tq, 1), NEG, F32), jnp.zeros((tq, 1), F32), jnp.zeros((tq, HEAD_DIM), F32))
        m, l, acc = lax.fori_loop(0, nch, body, init)
        o_ref[:, sl] = (acc / l).astype(BF)


def prompt_attention_b(qb, qi, wi, kb, vb, ki2, topk, tq):
    b, t, _ = qb.shape
    kern = functools.partial(_attn_b_kernel, tq=tq, topk=topk)
    qspec = lambda w: pl.BlockSpec((None, tq, w), lambda bi, i: (bi, i, 0))
    kspec = lambda w: pl.BlockSpec((None, t, w), lambda bi, i: (bi, 0, 0))
    return pl.pallas_call(
        kern,
        grid=(b, t // tq),
        in_specs=[qspec(W_B), qspec(W_QI), qspec(LANE), kspec(W_B), kspec(W_B), kspec(2 * LANE)],
        out_specs=qspec(W_B),
        out_shape=jax.ShapeDtypeStruct((b, t, W_B), BF),
        scratch_shapes=[pltpu.VMEM((t // tq, tq, tq), I32), pltpu.VMEM((t // tq, tq, tq), F32)],
        compiler_params=_cparams(("parallel", "arbitrary")),
        name="prompt_attention_b",
    )(qb, qi, wi, kb, vb, ki2)


ROWS = 8


def _page_heads(page_refs, kv, h):
    return [r[:, kv, h, :].astype(BF) for r in page_refs]


def _paged_softmax_kernel(pt_ref, *refs, n_groups, group, heads, kind, lam_init, t_new):
    del pt_ref
    if kind == "a":
        lam_ref, gain_ref, q_ref, kn_ref, vn_ref = refs[:5]
        rest = refs[5:]
        mask_ref = None
    else:
        q_ref, kn_ref, vn_ref, mask_ref = refs[:4]
        rest = refs[4:]
    page_refs = rest[:group]
    o_ref, m_sc, l_sc, acc_sc = rest[group:]
    j = pl.program_id(1)
    scale = (DA_QK if kind == "a" else HEAD_DIM) ** -0.5

    @pl.when(j == 0)
    def _():
        m_sc[...] = jnp.full(m_sc.shape, NEG, F32)
        l_sc[...] = jnp.zeros(l_sc.shape, F32)
        acc_sc[...] = jnp.zeros(acc_sc.shape, F32)

    def update(h, s, v):
        m, l, acc = _flash_update(s, v, m_sc[h][:, 0:1], l_sc[h][:, 0:1], acc_sc[h])
        m_sc[h] = jnp.broadcast_to(m, (ROWS, LANE))
        l_sc[h] = jnp.broadcast_to(l, (ROWS, LANE))
        acc_sc[h] = acc

    @pl.when(j < n_groups)
    def _():
        for h in range(heads):
            q = q_ref[h]
            ks = _page_heads(page_refs, 0, h)
            vs = _page_heads(page_refs, 1, h)
            s = jnp.concatenate([_dot_nt(q, k) for k in ks], axis=1) * scale
            if mask_ref is not None:
                s = s + jnp.concatenate([mask_ref[g] for g in range(group)], axis=1)
            update(h, s, jnp.concatenate(vs, axis=0))

    @pl.when(j == n_groups)
    def _():
        if kind == "a":
            lam = _lambda(lam_ref, lam_init)
            row = _iota((ROWS, LANE), 0)
            ok = _iota((ROWS, LANE), 1) <= jnp.where(row >= t_new, row - t_new, row)
            new_mask = jnp.where(ok, 0.0, NEG)
        else:
            new_mask = mask_ref[0]
        for h in range(heads):
            sl = slice(h * HEAD_DIM, (h + 1) * HEAD_DIM)
            s = _dot_nt(q_ref[h], kn_ref[:, sl]) * scale + new_mask
            update(h, s, vn_ref[:, sl])
            if kind == "a":
                o_ref[h] = _diff_finish(acc_sc[h], l_sc[h][:, 0:1], lam, gain_ref[:, sl], lam_init, t_new)
            else:
                o_ref[h] = acc_sc[h] / l_sc[h][:, 0:1]


def sample_paged_softmax(kind, page_table, cache, layer, q, k_new, v_new, *, mask=None,
                         lam_params=None, gain=None, lam_init=0.0, t_new=4, group=4):
    bsz, heads = q.shape[0], q.shape[1]
    n_pages = page_table.shape[1]
    n_groups = n_pages // group
    hw = heads * HEAD_DIM
    kern = functools.partial(_paged_softmax_kernel, n_groups=n_groups, group=group, heads=heads,
                             kind=kind, lam_init=lam_init, t_new=t_new)

    def page_spec(g):
        return pl.BlockSpec(
            (None, None, LANE, 2, heads, HEAD_DIM),
            lambda b, j, pt: (layer, pt[b, jnp.minimum(j * group + g, n_pages - 1)], 0, 0, 0, 0))

    in_specs = []
    args = []
    if kind == "a":
        in_specs += [pl.BlockSpec((None, 4, DA_QK), lambda b, j, pt: (layer, 0, 0)),
                     pl.BlockSpec((None, 1, hw), lambda b, j, pt: (layer, 0, 0))]
        args += [lam_params, gain]
    in_specs += [pl.BlockSpec((None, heads, ROWS, HEAD_DIM), lambda b, j, pt: (b, 0, 0, 0)),
                 pl.BlockSpec((None, LANE, hw), lambda b, j, pt: (b, 0, 0)),
                 pl.BlockSpec((None, LANE, hw), lambda b, j, pt: (b, 0, 0))]
    args += [q, k_new, v_new]
    if kind != "a":
        in_specs.append(pl.BlockSpec((None, group, ROWS, LANE), lambda b, j, pt: (b, j, 0, 0)))
        args.append(mask)
    in_specs += [page_spec(g) for g in range(group)]
    args += [cache] * group
    return pl.pallas_call(
        kern,
        grid_spec=pltpu.PrefetchScalarGridSpec(
            num_scalar_prefetch=1,
            grid=(bsz, n_groups + 1),
            in_specs=in_specs,
            out_specs=pl.BlockSpec((None, heads, ROWS, HEAD_DIM), lambda b, j, pt: (b, 0, 0, 0)),
            scratch_shapes=[pltpu.VMEM((heads, ROWS, LANE), F32)] * 3,
        ),
        out_shape=jax.ShapeDtypeStruct((bsz, heads, ROWS, HEAD_DIM), F32),
        compiler_params=_cparams(("parallel", "arbitrary")),
        name="sample_paged_softmax_" + kind,
    )(page_table, *args)


def _paged_stick_kernel(pt_ref, q_ref, kn_ref, vn_ref, *rest, group, heads, t_new):
    del pt_ref
    page_refs = rest[:group]
    o_ref, run_sc, acc_sc = rest[group:]
    j = pl.program_id(1)
    scale = HEAD_DIM ** -0.5
    mat = _suffix_matrix(LANE)

    def step(h, k, v, before):
        run, acc = _stick_chunk(q_ref[h], k, v, before, mat, run_sc[h][:, 0:1], acc_sc[h], scale)
        run_sc[h] = jnp.broadcast_to(run, (ROWS, LANE))
        acc_sc[h] = acc

    @pl.when(j == 0)
    def _():
        run_sc[...] = jnp.zeros(run_sc.shape, F32)
        acc_sc[...] = jnp.zeros(acc_sc.shape, F32)
        before = _iota((ROWS, LANE), 1) < jnp.minimum(_iota((ROWS, LANE), 0), t_new)
        for h in range(heads):
            sl = slice(h * HEAD_DIM, (h + 1) * HEAD_DIM)
            step(h, kn_ref[:, sl], vn_ref[:, sl], before)

    @pl.when(j > 0)
    def _():
        everything = _iota((ROWS, LANE), 1) >= 0
        for h in range(heads):
            for r in page_refs:
                step(h, r[:, 0, h, :].astype(BF), r[:, 1, h, :].astype(BF), everything)

    @pl.when(j == pl.num_programs(1) - 1)
    def _():
        o_ref[...] = acc_sc[...]


def sample_paged_stick(page_table, cache, layer, q, k_new, v_new, *, t_new=4, group=4):
    bsz, heads = q.shape[0], q.shape[1]
    n_pages = page_table.shape[1]
    n_groups = n_pages // group
    hw = heads * HEAD_DIM
    kern = functools.partial(_paged_stick_kernel, group=group, heads=heads, t_new=t_new)

    def page_spec(g):
        return pl.BlockSpec(
            (None, None, LANE, 2, heads, HEAD_DIM),
            lambda b, j, pt: (layer, pt[b, n_pages - 1 - (jnp.maximum(j - 1, 0) * group + g)], 0, 0, 0, 0))

    in_specs = [pl.BlockSpec((None, heads, ROWS, HEAD_DIM), lambda b, j, pt: (b, 0, 0, 0)),
                pl.BlockSpec((None, LANE, hw), lambda b, j, pt: (b, 0, 0)),
                pl.BlockSpec((None, LANE, hw), lambda b, j, pt: (b, 0, 0))]
    in_specs += [page_spec(g) for g in range(group)]
    return pl.pallas_call(
        kern,
        grid_spec=pltpu.PrefetchScalarGridSpec(
            num_scalar_prefetch=1,
            grid=(bsz, n_groups + 1),
            in_specs=in_specs,
            out_specs=pl.BlockSpec((None, heads, ROWS, HEAD_DIM), lambda b, j, pt: (b, 0, 0, 0)),
            scratch_shapes=[pltpu.VMEM((heads, ROWS, LANE), F32)] * 2,
        ),
        out_shape=jax.ShapeDtypeStruct((bsz, heads, ROWS, HEAD_DIM), F32),
        compiler_params=_cparams(("parallel", "arbitrary")),
        name="sample_paged_stick",
    )(page_table, q, k_new, v_new, *([cache] * group))


def _sample_score_kernel(pt_ref, qi_ref, wi_ref, kin_ref, *rest, n_groups, group, t_new):
    del pt_ref
    page_refs = rest[:group]
    o_ref = rest[group]
    j = pl.program_id(1)
    qi = qi_ref[...]
    wi = wi_ref[...]

    def score(ki):
        d = jnp.maximum(_dot_nt(qi, ki) * (D_IDX ** -0.5), 0.0) * wi
        return jnp.sum(d.reshape(H_IDX, ROWS, LANE), axis=0)

    @pl.when(j < n_groups)
    def _():
        for g in range(group):
            o_ref[g] = score(page_refs[g][...].astype(BF))

    @pl.when(j == n_groups)
    def _():
        ok = (_iota((ROWS, LANE), 1) <= _iota((ROWS, LANE), 0)) & (_iota((ROWS, LANE), 1) < t_new)
        o_ref[0] = jnp.where(ok, score(kin_ref[...]), -jnp.inf)
        for g in range(1, group):
            o_ref[g] = jnp.full((ROWS, LANE), -jnp.inf, F32)


def sample_indexer_scores(page_table, cache_idx, layer, qi, wi, ki_new, *, t_new=4, group=4):
    bsz = qi.shape[0]
    n_pages = page_table.shape[1]
    n_groups = n_pages // group
    kern = functools.partial(_sample_score_kernel, n_groups=n_groups, group=group, t_new=t_new)

    def page_spec(g):
        return pl.BlockSpec(
            (None, None, LANE, D_IDX),
            lambda b, j, pt: (layer, pt[b, jnp.minimum(j * group + g, n_pages - 1)], 0, 0))

    in_specs = [pl.BlockSpec((None, H_IDX * ROWS, D_IDX), lambda b, j, pt: (b, 0, 0)),
                pl.BlockSpec((None, H_IDX * ROWS, 1), lambda b, j, pt: (b, 0, 0)),
                pl.BlockSpec((None, LANE, D_IDX), lambda b, j, pt: (b, 0, 0))]
    in_specs += [page_spec(g) for g in range(group)]
    return pl.pallas_call(
        kern,
        grid_spec=pltpu.PrefetchScalarGridSpec(
            num_scalar_prefetch=1,
            grid=(bsz, n_groups + 1),
            in_specs=in_specs,
            out_specs=pl.BlockSpec((None, group, ROWS, LANE), lambda b, j, pt: (b, j, 0, 0)),
        ),
        out_shape=jax.ShapeDtypeStruct((bsz, (n_groups + 1) * group, ROWS, LANE), F32),
        compiler_params=_cparams(("parallel", "arbitrary")),
        name="sample_indexer_scores",
    )(page_table, qi, wi, ki_new, *([cache_idx] * group))


def _sample_select_kernel(s_ref, o_ref, key_sc, *, topk):
    nch = s_ref.shape[0]

    def key_body(c, _):
        key_sc[c] = _sort_key(s_ref[c])
        return 0

    lax.fori_loop(0, nch, key_body, 0)

    def count(pred):
        cnt = lax.fori_loop(0, nch, lambda c, a: a + jnp.where(pred(key_sc[c]), 1.0, 0.0),
                            jnp.zeros((ROWS, LANE), F32))
        return jnp.sum(cnt, axis=1, keepdims=True)

    t = _kth_largest_key(lambda cand: count(lambda k: k >= cand), (ROWS, 1), topk)
    need = float(topk) - count(lambda k: k > t)
    pmat = _prefix_matrix(LANE)

    def sel_body(c, carry):
        o_ref[c], carry = _select_mask(key_sc[c], t, need, carry, pmat)
        return carry

    lax.fori_loop(0, nch, sel_body, jnp.zeros((ROWS, 1), F32))


def sample_select_mask(scores, topk):
    bsz, nch = scores.shape[:2]
    blk = pl.BlockSpec((None, nch, ROWS, LANE), lambda b: (b, 0, 0, 0))
    return pl.pallas_call(
        functools.partial(_sample_select_kernel, topk=topk),
        grid=(bsz,),
        in_specs=[blk],
        out_specs=blk,
        out_shape=jax.ShapeDtypeStruct(scores.shape, F32),
        scratch_shapes=[pltpu.VMEM((nch, ROWS, LANE), I32)],
        compiler_params=_cparams(("parallel",)),
        name="sample_select_mask",
    )(scores)


def _layer_norm(x, g, b):
    mu = jnp.mean(x, axis=1, keepdims=True)
    cen = x - mu
    var = jnp.mean(cen * cen, axis=1, keepdims=True)
    return cen * lax.rsqrt(var + LN_EPS) * g + b


def _out_kernel(oa_ref, ob_ref, oc_ref, w_ref, x_ref, ga_ref, lng_ref, lnb_ref, sc_ref, sh_ref,
                wr_ref, br_ref, x1_ref, h_ref, lg_ref, *, alpha):
    attn = (_dot(oa_ref[...], w_ref[0:W_A, :])
            + _dot(ob_ref[...], w_ref[W_A:W_A + W_B, :])
            + _dot(oc_ref[...], w_ref[W_A + W_B:W_A + W_B + W_C, :]))
    x1 = _layer_norm(alpha * x_ref[...] + (1.0 + ga_ref[...]) * attn, lng_ref[...], lnb_ref[...])
    x1_ref[...] = x1
    h = x1 * (1.0 + sc_ref[...]) + sh_ref[...]
    h_ref[...] = h.astype(BF)
    hh = h.astype(BF)
    hl = (h - hh.astype(F32)).astype(BF)
    w = wr_ref[...]
    wh = w.astype(BF)
    wl = (w - wh.astype(F32)).astype(BF)
    lg_ref[...] = _dot(hh, wh) + _dot(hh, wl) + _dot(hl, wh) + br_ref[...]


def out_projection(oa, ob, oc, w_out, x, g_a, ln_g, ln_b, sc_f, sh_f, w_router, b_router, layer, alpha, tm):
    b, t, d = x.shape
    ts = g_a.shape[1]
    tms = 1 if ts == 1 else tm
    mod_map = (lambda bi, i: (bi, 0, 0)) if ts == 1 else (lambda bi, i: (bi, i, 0))
    ne = w_router.shape[-1]
    row = lambda w: pl.BlockSpec((None, tm, w), lambda bi, i: (bi, i, 0))
    mod = pl.BlockSpec((None, tms, d), mod_map)
    par = lambda w: pl.BlockSpec((None, 1, w), lambda bi, i: (layer, 0, 0))
    return pl.pallas_call(
        functools.partial(_out_kernel, alpha=alpha),
        grid=(b, t // tm),
        in_specs=[row(W_A), row(W_B), row(W_C),
                  pl.BlockSpec((None, w_out.shape[1], d), lambda bi, i: (layer, 0, 0)),
                  row(d), mod, par(d), par(d), mod, mod,
                  pl.BlockSpec((None, d, ne), lambda bi, i: (layer, 0, 0)), par(ne)],
        out_specs=[row(d), row(d), row(ne)],
        out_shape=[jax.ShapeDtypeStruct((b, t, d), F32), jax.ShapeDtypeStruct((b, t, d), BF),
                   jax.ShapeDtypeStruct((b, t, ne), F32)],
        compiler_params=_cparams(("parallel", "parallel")),
        name="out_projection",
    )(oa, ob, oc, w_out, x, g_a, ln_g, ln_b, sc_f, sh_f, w_router, b_router)


def _router_kernel(lg_ref, idx_ref, w_ref):
    lg = lg_ref[...]
    n, ne = lg.shape
    lane = _iota((n, ne), 1).astype(F32)
    vals, idxs = [], []
    for _ in range(TOP_K_EXPERTS):
        mx = jnp.max(lg, axis=1, keepdims=True)
        ix = jnp.min(jnp.where(lg == mx, lane, float(ne)), axis=1, keepdims=True)
        vals.append(mx)
        idxs.append(ix)
        lg = jnp.where(lane == ix, -jnp.inf, lg)
    es = [jnp.exp(v - vals[0]) for v in vals]
    tot = es[0]
    for e in es[1:]:
        tot = tot + e
    out_lane = _iota((n, TOP_K_EXPERTS), 1)
    idx = jnp.zeros((n, TOP_K_EXPERTS), F32)
    wgt = jnp.zeros((n, TOP_K_EXPERTS), F32)
    for k in range(TOP_K_EXPERTS):
        idx = jnp.where(out_lane == k, idxs[k], idx)
        wgt = jnp.where(out_lane == k, es[k] / tot, wgt)
    idx_ref[...] = idx.astype(I32)
    w_ref[...] = wgt


def router_topk(logits, tm):
    n, ne = logits.shape
    return pl.pallas_call(
        _router_kernel,
        grid=(n // tm,),
        in_specs=[pl.BlockSpec((tm, ne), lambda i: (i, 0))],
        out_specs=[pl.BlockSpec((tm, TOP_K_EXPERTS), lambda i: (i, 0))] * 2,
        out_shape=[jax.ShapeDtypeStruct((n, TOP_K_EXPERTS), I32), jax.ShapeDtypeStruct((n, TOP_K_EXPERTS), F32)],
        compiler_params=_cparams(("parallel",)),
        name="router_topk",
    )(logits)


def _gate_up_kernel(te_ref, tv_ref, x_ref, wg_ref, wu_ref, bg_ref, bu_ref, o_ref, wg_sc, wu_sc):
    i = pl.program_id(1)
    changed = jnp.logical_or(i == 0, te_ref[i] != te_ref[jnp.maximum(i - 1, 0)])

    @pl.when(changed)
    def _():
        wg_sc[...] = wg_ref[...].astype(BF)
        wu_sc[...] = wu_ref[...].astype(BF)

    @pl.when(tv_ref[i] > 0)
    def _():
        x = x_ref[...]
        g = jnp.minimum(_dot(x, wg_sc[...]) + bg_ref[...], SWIGLU_LIMIT)
        u = jnp.clip(_dot(x, wu_sc[...]) + bu_ref[...], -SWIGLU_LIMIT, SWIGLU_LIMIT)
        o_ref[...] = ((u + 1.0) * g * jax.nn.sigmoid(SWIGLU_ALPHA * g)).astype(BF)

    @pl.when(tv_ref[i] == 0)
    def _():
        o_ref[...] = jnp.zeros(o_ref.shape, BF)


def expert_gate_up(tile_expert, tile_valid, xs, w_gate_up, b_gate_up, layer, tm, tn):
    r, d = xs.shape
    dff = w_gate_up.shape[-1] // 2
    nj = dff // tn
    ne = w_gate_up.shape[1]
    bias = b_gate_up.reshape(b_gate_up.shape[0], ne, 1, 2 * dff)
    return pl.pallas_call(
        _gate_up_kernel,
        grid_spec=pltpu.PrefetchScalarGridSpec(
            num_scalar_prefetch=2,
            grid=(nj, r // tm),
            in_specs=[
                pl.BlockSpec((tm, d), lambda j, i, te, tv: (i, 0)),
                pl.BlockSpec((None, None, d, tn), lambda j, i, te, tv: (layer, te[i], 0, j)),
                pl.BlockSpec((None, None, d, tn), lambda j, i, te, tv: (layer, te[i], 0, j + nj)),
                pl.BlockSpec((None, None, 1, tn), lambda j, i, te, tv: (layer, te[i], 0, j)),
                pl.BlockSpec((None, None, 1, tn), lambda j, i, te, tv: (layer, te[i], 0, j + nj)),
            ],
            out_specs=pl.BlockSpec((tm, tn), lambda j, i, te, tv: (i, j)),
            scratch_shapes=[pltpu.VMEM((d, tn), BF)] * 2,
        ),
        out_shape=jax.ShapeDtypeStruct((r, dff), BF),
        compiler_params=_cparams(("arbitrary", "arbitrary")),
        name="expert_gate_up",
    )(tile_expert, tile_valid, xs, w_gate_up, w_gate_up, bias, bias)


def _down_kernel(te_ref, tv_ref, a_ref, w_ref, b_ref, o_ref, w_sc):
    i = pl.program_id(1)
    changed = jnp.logical_or(i == 0, te_ref[i] != te_ref[jnp.maximum(i - 1, 0)])

    @pl.when(changed)
    def _():
        w_sc[...] = w_ref[...].astype(BF)

    @pl.when(tv_ref[i] > 0)
    def _():
        o_ref[...] = _dot(a_ref[...], w_sc[...]) + b_ref[...]

    @pl.when(tv_ref[i] == 0)
    def _():
        o_ref[...] = jnp.zeros(o_ref.shape, F32)


def expert_down(tile_expert, tile_valid, act, w_down, b_down, layer, tm, tn):
    r, dff = act.shape
    d = w_down.shape[-1]
    ne = w_down.shape[1]
    bias = b_down.reshape(b_down.shape[0], ne, 1, d)
    return pl.pallas_call(
        _down_kernel,
        grid_spec=pltpu.PrefetchScalarGridSpec(
            num_scalar_prefetch=2,
            grid=(d // tn, r // tm),
            in_specs=[
                pl.BlockSpec((tm, dff), lambda j, i, te, tv: (i, 0)),
                pl.BlockSpec((None, None, dff, tn), lambda j, i, te, tv: (layer, te[i], 0, j)),
                pl.BlockSpec((None, None, 1, tn), lambda j, i, te, tv: (layer, te[i], 0, j)),
            ],
            out_specs=pl.BlockSpec((tm, tn), lambda j, i, te, tv: (i, j)),
            scratch_shapes=[pltpu.VMEM((dff, tn), BF)],
        ),
        out_shape=jax.ShapeDtypeStruct((r, d), F32),
        compiler_params=_cparams(("arbitrary", "arbitrary")),
        name="expert_down",
    )(tile_expert, tile_valid, act, w_down, bias)


def _ffn_norm_kernel(x_ref, y_ref, w_ref, g_ref, lng_ref, lnb_ref, o_ref, *, alpha):
    w = w_ref[...]
    ffn = y_ref[0] * w[:, 0:1]
    for k in range(1, TOP_K_EXPERTS):
        ffn = ffn + y_ref[k] * w[:, k:k + 1]
    o_ref[...] = _layer_norm(alpha * x_ref[...] + (1.0 + g_ref[...]) * ffn, lng_ref[...], lnb_ref[...])


def ffn_combine_norm(x1, y4, top_w, g_f, ln_g, ln_b, layer, alpha, tm):
    n, d = x1.shape
    row = pl.BlockSpec((tm, d), lambda i: (i, 0))
    par = pl.BlockSpec((None, 1, d), lambda i: (layer, 0, 0))
    return pl.pallas_call(
        functools.partial(_ffn_norm_kernel, alpha=alpha),
        grid=(n // tm,),
        in_specs=[row, pl.BlockSpec((TOP_K_EXPERTS, tm, d), lambda i: (0, i, 0)),
                  pl.BlockSpec((tm, TOP_K_EXPERTS), lambda i: (i, 0)), row, par, par],
        out_specs=row,
        out_shape=jax.ShapeDtypeStruct((n, d), F32),
        compiler_params=_cparams(("parallel",)),
        name="ffn_combine_norm",
    )(x1, y4, top_w, g_f, ln_g, ln_b)


def _route(top_idx, n_experts, tm):
    n, k = top_idx.shape
    flat = top_idx.reshape(-1)
    order = jnp.argsort(flat, stable=True)
    sorted_e = flat[order]
    sizes = jnp.zeros((n_experts,), I32).at[flat].add(1)
    starts = jnp.cumsum(sizes) - sizes
    padded = ((sizes + tm - 1) // tm) * tm
    pstarts = jnp.cumsum(padded) - padded
    n_rows = ((n * k + n_experts * (tm - 1)) // tm + 1) * tm
    rank = jnp.arange(n * k, dtype=I32) - starts[sorted_e]
    dest_sorted = pstarts[sorted_e] + rank
    src_token = jnp.zeros((n_rows,), I32).at[dest_sorted].set((order // k).astype(I32))
    dest = jnp.zeros((n * k,), I32).at[order].set(dest_sorted).reshape(n, k)
    tile_start = jnp.arange(n_rows // tm, dtype=I32) * tm
    pend = pstarts + padded
    te = jnp.minimum(jnp.searchsorted(pend, tile_start, side="right"), n_experts - 1).astype(I32)
    tv = (tile_start < pend[-1]).astype(I32)
    return src_token, dest, te, tv


def moe_ffn(h_tok, logits, x1, g_f, w_gate_up, b_gate_up, w_down, b_down, ln_g, ln_b, layer, alpha,
            tm_route, tm_tok):
    n, d = x1.shape
    ne = w_gate_up.shape[1]
    dff = w_down.shape[2]
    top_idx, top_w = router_topk(logits, tm_tok)
    src_token, dest, te, tv = _route(top_idx, ne, tm_route)
    xs = jnp.take(h_tok, src_token, axis=0)
    act = expert_gate_up(te, tv, xs, w_gate_up, b_gate_up, layer, tm_route, min(512, dff))
    y = expert_down(te, tv, act, w_down, b_down, layer, tm_route, min(512, d))
    y4 = jnp.take(y, dest.T, axis=0)
    return ffn_combine_norm(x1, y4, top_w, g_f, ln_g, ln_b, layer, alpha, tm_tok)


def _pad_w_in(w_in):
    depth, d, _ = w_in.shape
    z = lambda n: jnp.zeros((depth, d, n), w_in.dtype)
    return jnp.concatenate([
        w_in[:, :, :S_WI], w_in[:, :, S_WI:S_KI], z(LANE - H_IDX),
        w_in[:, :, S_KI:S_QC], z(LANE - D_IDX), w_in[:, :, S_QC:]], axis=2).astype(BF)


def _pick(n, prefs):
    for p in prefs:
        if n % p == 0:
            return p
    return n


def kernel(x_prompt, x_sample, cache_kv_a, cache_kv_b, cache_idx_b, cache_kv_c, page_table, c_prompt, c_sample,
           w_ada, b_ada, w_in, ln_idx_g, ln_idx_b, lam_params, gn_a_gain, w_out, ln1_g, ln1_b, w_router,
           b_router, w_gate_up, b_gate_up, w_down, b_down, ln2_g, ln2_b):
    depth = w_in.shape[0]
    bp, tp, d = x_prompt.shape
    bs, t_new, _ = x_sample.shape
    n_pages = page_table.shape[1]
    past_len = n_pages * cache_kv_a.shape[2]
    alpha = (2 * depth) ** 0.25
    topk_p = min(TOPK_MAX, tp // 4)
    topk_s = min(TOPK_MAX, (past_len + t_new) // 4)
    ne = w_router.shape[-1]
    group = _pick(n_pages, (4, 2, 1))

    w_in_p = _pad_w_in(w_in)
    w_out_b = w_out.astype(BF)
    pad_lane = lambda v: jnp.pad(v, ((0, 0), (0, LANE - v.shape[1])))
    ln_gb = jnp.stack([pad_lane(ln_idx_g), pad_lane(ln_idx_b)], axis=1)
    gain = gn_a_gain.reshape(depth, 1, W_A)
    vec = lambda v: v.reshape(depth, 1, -1)

    pos_p = jnp.arange(tp, dtype=I32)
    pos_s = past_len + jnp.arange(t_new, dtype=I32)
    tab64_p = jnp.tile(rope_tables(pos_p, DA_QK), (bp, 1))
    tab128_p = jnp.tile(rope_tables(pos_p, HEAD_DIM), (bp, 1))
    tab64_s = jnp.tile(rope_tables(pos_s, DA_QK), (bs, 1))
    tab128_s = jnp.tile(rope_tables(pos_s, HEAD_DIM), (bs, 1))

    n_c = bp + bs
    rows_c = -(-n_c // 8) * 8
    c_all = jnp.pad(jnp.concatenate([c_prompt, c_sample], axis=0), ((0, rows_c - n_c), (0, 0)))
    mod = ada_modulation(c_all, w_ada, b_ada, tn=_pick(6 * d, (1024, 512, 256, 128))).reshape(depth, rows_c, 6, d)

    tm_proj = _pick(tp, (1024, 512, 256, 128))
    tm_epi = _pick(tp, (256, 128))
    tq_a = _pick(tp, (256, 128))
    tm_out = _pick(tp, (256, 128))
    ms = bs * t_new
    n_tok = bp * tp + ms
    tm_tok = _pick(n_tok, (256, 128, 32, 8))

    xp, xs = x_prompt, x_sample.reshape(1, ms, d)
    outs_p, outs_s = [], []
    first = jnp.arange(HEAD_DIM) < DA_QK
    for l in range(depth):
        lam_init = 0.8 - 0.6 * math.exp(-0.3 * l)
        mp = mod[l, :bp][:, :, None, :]
        msr = jnp.repeat(mod[l, bp:n_c], t_new, axis=0)[None]
        sh_a, sc_a, g_a, sh_f, sc_f, g_f = [mp[:, i] for i in range(6)]
        sh_as, sc_as, g_as, sh_fs, sc_fs, g_fs = [msr[:, :, i] for i in range(6)]

        proj = modulated_projection(xp, sc_a, sh_a, w_in_p, l, tm_proj).reshape(bp * tp, N_PAD)
        (kv_a, kv_b, idx, kv_c, qa, ka, va, qb, kb, vb, qi, wi, ki2, qc, kc, vc) = projection_epilogue(
            proj, tab64_p, tab128_p, ln_gb[l], tm_epi)
        r3 = lambda a: a.reshape(bp, tp, a.shape[-1])
        oa = prompt_attention_a(r3(qa), r3(ka), r3(va), lam_params, gain, l, lam_init, tq_a, min(tq_a, 256))
        ob = prompt_attention_b(r3(qb), r3(qi), r3(wi), r3(kb), r3(vb), r3(ki2), topk_p, LANE)
        oc = prompt_attention_c(r3(qc), r3(kc), r3(vc), tq_a, LANE)
        x1p, hp, lgp = out_projection(oa, ob, oc, w_out_b, xp, g_a, vec(ln1_g), vec(ln1_b), sc_f, sh_f,
                                      w_router, vec(b_router), l, alpha, tm_out)
        outs_p.append((kv_a.reshape(bp, tp, 2, H_A, HEAD_DIM), kv_b.reshape(bp, tp, 2, H_B, HEAD_DIM),
                       idx.reshape(bp, tp, D_IDX), kv_c.reshape(bp, tp, 2, H_C, HEAD_DIM)))

        proj_s = modulated_projection(xs, sc_as, sh_as, w_in_p, l, ms).reshape(ms, N_PAD)
        (kv_a_s, kv_b_s, idx_s, kv_c_s, qa_s, ka_s, va_s, qb_s, kb_s, vb_s, qi_s, wi_s, ki2_s, qc_s, kc_s,
         vc_s) = projection_epilogue(proj_s, tab64_s, tab128_s, ln_gb[l], ms)
        new_rows = lambda a: jnp.pad(a.reshape(bs, t_new, a.shape[-1]), ((0, 0), (0, LANE - t_new), (0, 0)))
        heads_first = lambda a, h: jnp.swapaxes(a.reshape(bs, t_new, h, HEAD_DIM), 1, 2)
        pad_rows = lambda a: jnp.pad(a, ((0, 0), (0, 0), (0, ROWS - t_new), (0, 0)))
        qa_h = heads_first(qa_s, H_A)
        qa_2 = jnp.concatenate([jnp.where(first, qa_h, 0), jnp.where(first, 0, qa_h)], axis=2).astype(BF)
        oa_s = sample_paged_softmax("a", page_table, cache_kv_a, l, qa_2, new_rows(ka_s), new_rows(va_s),
                                    lam_params=lam_params, gain=gain, lam_init=lam_init, t_new=t_new, group=group)
        qi_h = pad_rows(jnp.swapaxes(qi_s.reshape(bs, t_new, H_IDX, D_IDX), 1, 2)).reshape(bs, H_IDX * ROWS, D_IDX)
        wi_h = pad_rows(jnp.swapaxes(wi_s[:, :H_IDX].reshape(bs, t_new, H_IDX, 1), 1, 2)).reshape(
            bs, H_IDX * ROWS, 1)
        scores = sample_indexer_scores(page_table, cache_idx_b, l, qi_h, wi_h, new_rows(ki2_s[:, :D_IDX]),
                                       t_new=t_new, group=group)
        sel = sample_select_mask(scores, topk_s)
        ob_s = sample_paged_softmax("b", page_table, cache_kv_b, l, pad_rows(heads_first(qb_s, H_B)),
                                    new_rows(kb_s), new_rows(vb_s), mask=sel, t_new=t_new, group=group)
        oc_s = sample_paged_stick(page_table, cache_kv_c, l, pad_rows(heads_first(qc_s, H_C)),
                                  new_rows(kc_s), new_rows(vc_s), t_new=t_new, group=group)
        tok = lambda o, h: jnp.swapaxes(o[:, :, :t_new], 1, 2).reshape(1, ms, h * HEAD_DIM).astype(BF)
        x1s, hs, lgs = out_projection(tok(oa_s, H_A), tok(ob_s, H_B), tok(oc_s, H_C), w_out_b, xs, g_as,
                                      vec(ln1_g), vec(ln1_b), sc_fs, sh_fs, w_router, vec(b_router), l, alpha, ms)
        outs_s.append((kv_a_s.reshape(bs, t_new, 2, H_A, HEAD_DIM), kv_b_s.reshape(bs, t_new, 2, H_B, HEAD_DIM),
                       idx_s.reshape(bs, t_new, D_IDX), kv_c_s.reshape(bs, t_new, 2, H_C, HEAD_DIM)))

        cat = lambda a, b_: jnp.concatenate([a.reshape(bp * tp, -1), b_.reshape(ms, -1)], axis=0)
        g_tok = jnp.concatenate([jnp.broadcast_to(g_f, (bp, tp, d)).reshape(bp * tp, d), g_fs.reshape(ms, d)], axis=0)
        x2 = moe_ffn(cat(hp, hs), cat(lgp, lgs), cat(x1p, x1s), g_tok, w_gate_up, b_gate_up, w_down, b_down,
                     vec(ln2_g), vec(ln2_b), l, alpha, 256, tm_tok)
        xp = x2[:bp * tp].reshape(bp, tp, d)
        xs = x2[bp * tp:].reshape(1, ms, d)

    stack = lambda outs, i: jnp.stack([o[i] for o in outs])
    return (xp, xs.reshape(bs, t_new, d),
            stack(outs_p, 0), stack(outs_p, 1), stack(outs_p, 2), stack(outs_p, 3),
            stack(outs_s, 0), stack(outs_s, 1), stack(outs_s, 2), stack(outs_s, 3))
```

```python
import functools
import math

import jax
import jax.numpy as jnp
from jax import lax
from jax.experimental import pallas as pl
from jax.experimental.pallas import tpu as pltpu

BF = jnp.bfloat16
F32 = jnp.float32
I32 = jnp.int32

HEAD_DIM = 128
H_A, H_B, H_C = 6, 5, 5
DA_QK = 64
H_IDX, D_IDX = 16, 64
TOPK_MAX = 256
ROPE_THETA = 500000.0
ROT_FRAC = 4
TOP_K_EXPERTS = 4
SWIGLU_LIMIT = 7.0
SWIGLU_ALPHA = 1.702
LN_EPS = 1e-5
RMS_EPS = 1e-6
LANE = 128
NEG = -1e30
KEY_NEG_INF = -2139095041
VMEM_LIMIT = 56 * 1024 * 1024

W_A = H_A * HEAD_DIM
W_B = H_B * HEAD_DIM
W_C = H_C * HEAD_DIM
W_QI = H_IDX * D_IDX
S_QA, S_KA, S_VA = 0, W_A, 2 * W_A
S_QB = 3 * W_A
S_KB, S_VB = S_QB + W_B, S_QB + 2 * W_B
S_QI = S_QB + 3 * W_B
S_WI = S_QI + W_QI
S_KI = S_WI + H_IDX
S_QC = S_KI + D_IDX
S_KC, S_VC = S_QC + W_C, S_QC + 2 * W_C
N_IN = S_QC + 3 * W_C
P_QA, P_KA, P_VA, P_QB, P_KB, P_VB, P_QI = S_QA, S_KA, S_VA, S_QB, S_KB, S_VB, S_QI
P_WI = P_QI + W_QI
P_KI = P_WI + LANE
P_QC = P_KI + LANE
P_KC, P_VC = P_QC + W_C, P_QC + 2 * W_C
N_PAD = P_QC + 3 * W_C


def _cparams(sem, vmem=VMEM_LIMIT):
    return pltpu.CompilerParams(dimension_semantics=sem, vmem_limit_bytes=vmem)


def _dot(a, b):
    return jnp.dot(a, b, preferred_element_type=F32)


def _dot_nt(a, b):
    return lax.dot_general(a, b, (((1,), (1,)), ((), ())), preferred_element_type=F32)


def _iota(shape, dim):
    return lax.broadcasted_iota(I32, shape, dim)


def _ada_kernel(c_ref, w_ref, b_ref, o_ref):
    c = c_ref[...]
    s = c * jax.nn.sigmoid(c)
    o_ref[...] = _dot(s, w_ref[...]) + b_ref[...]


def ada_modulation(c_all, w_ada, b_ada, tn=1024):
    depth, d, n = w_ada.shape
    rows = c_all.shape[0]
    return pl.pallas_call(
        _ada_kernel,
        grid=(depth, n // tn),
        in_specs=[
            pl.BlockSpec((rows, d), lambda l, j: (0, 0)),
            pl.BlockSpec((None, d, tn), lambda l, j: (l, 0, j)),
            pl.BlockSpec((None, 1, tn), lambda l, j: (l, 0, j)),
        ],
        out_specs=pl.BlockSpec((None, rows, tn), lambda l, j: (l, 0, j)),
        out_shape=jax.ShapeDtypeStruct((depth, rows, n), F32),
        compiler_params=_cparams(("parallel", "parallel")),
        name="ada_modulation",
    )(c_all, w_ada, b_ada.reshape(depth, 1, n))


def _proj_kernel(x_ref, sc_ref, sh_ref, w_ref, o_ref, h_sc):
    @pl.when(pl.program_id(2) == 0)
    def _():
        h_sc[...] = (x_ref[...] * (1.0 + sc_ref[...]) + sh_ref[...]).astype(BF)

    o_ref[...] = _dot(h_sc[...], w_ref[...])


def modulated_projection(x, sc, sh, w, layer, tm, tn=256):
    b, t, d = x.shape
    n = w.shape[-1]
    ts = sc.shape[1]
    tms = 1 if ts == 1 else tm
    mod_map = (lambda bi, i, j: (bi, 0, 0)) if ts == 1 else (lambda bi, i, j: (bi, i, 0))
    return pl.pallas_call(
        _proj_kernel,
        grid=(b, t // tm, n // tn),
        in_specs=[
            pl.BlockSpec((None, tm, d), lambda bi, i, j: (bi, i, 0)),
            pl.BlockSpec((None, tms, d), mod_map),
            pl.BlockSpec((None, tms, d), mod_map),
            pl.BlockSpec((None, d, tn), lambda bi, i, j: (layer, 0, j)),
        ],
        out_specs=pl.BlockSpec((None, tm, tn), lambda bi, i, j: (bi, i, j)),
        out_shape=jax.ShapeDtypeStruct((b, t, n), F32),
        scratch_shapes=[pltpu.VMEM((tm, d), BF)],
        compiler_params=_cparams(("parallel", "parallel", "arbitrary")),
        name="modulated_projection",
    )(x, sc, sh, w)


def _rope(x, tab_ref, half):
    c = tab_ref[:, 0:LANE]
    sa = tab_ref[:, LANE:2 * LANE]
    sb = tab_ref[:, 2 * LANE:3 * LANE]
    outs = []
    for blk in range(x.shape[1] // LANE):
        xb = x[:, blk * LANE:(blk + 1) * LANE]
        outs.append(xb * c + pltpu.roll(xb, half, 1) * sa + pltpu.roll(xb, LANE - half, 1) * sb)
    return outs[0] if len(outs) == 1 else jnp.concatenate(outs, axis=1)


def _epilogue_kernel(p_ref, t64_ref, t128_ref, ln_ref,
                     kva_ref, kvb_ref, idx_ref, kvc_ref,
                     qa_ref, ka_ref, va_ref, qb_ref, kb_ref, vb_ref,
                     qi_ref, wi_ref, ki2_ref, qc_ref, kc_ref, vc_ref):
    h64 = DA_QK // ROT_FRAC // 2
    h128 = HEAD_DIM // ROT_FRAC // 2
    qa = _rope(p_ref[:, P_QA:P_QA + W_A], t64_ref, h64)
    ka = _rope(p_ref[:, P_KA:P_KA + W_A], t64_ref, h64)
    va = p_ref[:, P_VA:P_VA + W_A]
    qa_ref[...] = qa.astype(BF)
    ka_ref[...] = ka.astype(BF)
    va_ref[...] = va.astype(BF)
    kva_ref[:, 0:W_A] = ka
    kva_ref[:, W_A:2 * W_A] = va

    qb = _rope(p_ref[:, P_QB:P_QB + W_B], t128_ref, h128)
    kb = _rope(p_ref[:, P_KB:P_KB + W_B], t128_ref, h128)
    vb = p_ref[:, P_VB:P_VB + W_B]
    qb_ref[...] = qb.astype(BF)
    kb_ref[...] = kb.astype(BF)
    vb_ref[...] = vb.astype(BF)
    kvb_ref[:, 0:W_B] = kb
    kvb_ref[:, W_B:2 * W_B] = vb

    qi_ref[...] = _rope(p_ref[:, P_QI:P_QI + W_QI], t64_ref, h64).astype(BF)
    wi_ref[...] = p_ref[:, P_WI:P_WI + LANE] * (H_IDX ** -0.5)

    ki = p_ref[:, P_KI:P_KI + LANE]
    real = _iota((1, LANE), 1) < D_IDX
    mu = jnp.sum(ki, axis=1, keepdims=True) * (1.0 / D_IDX)
    cen = jnp.where(real, ki - mu, 0.0)
    var = jnp.sum(cen * cen, axis=1, keepdims=True) * (1.0 / D_IDX)
    kin = cen * lax.rsqrt(var + LN_EPS) * ln_ref[0:1, :] + ln_ref[1:2, :]
    kir = jnp.where(real, _rope(kin, t64_ref, h64), 0.0)
    idx_ref[...] = kir[:, 0:D_IDX]
    kib = kir.astype(BF)
    ki2_ref[:, 0:LANE] = kib
    ki2_ref[:, LANE:2 * LANE] = pltpu.roll(kir, D_IDX, 1).astype(BF)

    kc = p_ref[:, P_KC:P_KC + W_C]
    vc = p_ref[:, P_VC:P_VC + W_C]
    qc_ref[...] = p_ref[:, P_QC:P_QC + W_C].astype(BF)
    kc_ref[...] = kc.astype(BF)
    vc_ref[...] = vc.astype(BF)
    kvc_ref[:, 0:W_C] = kc
    kvc_ref[:, W_C:2 * W_C] = vc


def projection_epilogue(proj, tab64, tab128, ln_gb, tm):
    m = proj.shape[0]
    row = lambda w: pl.BlockSpec((tm, w), lambda i: (i, 0))
    shapes = [
        ((m, 2 * W_A), F32), ((m, 2 * W_B), F32), ((m, D_IDX), F32), ((m, 2 * W_C), F32),
        ((m, W_A), BF), ((m, W_A), BF), ((m, W_A), BF),
        ((m, W_B), BF), ((m, W_B), BF), ((m, W_B), BF),
        ((m, W_QI), BF), ((m, LANE), F32), ((m, 2 * LANE), BF),
        ((m, W_C), BF), ((m, W_C), BF), ((m, W_C), BF),
    ]
    return pl.pallas_call(
        _epilogue_kernel,
        grid=(m // tm,),
        in_specs=[row(N_PAD), row(3 * LANE), row(3 * LANE), pl.BlockSpec((2, LANE), lambda i: (0, 0))],
        out_specs=[row(s[0][1]) for s in shapes],
        out_shape=[jax.ShapeDtypeStruct(*s) for s in shapes],
        compiler_params=_cparams(("parallel",)),
        name="projection_epilogue",
    )(proj, tab64, tab128, ln_gb)


def rope_tables(pos, head_dim):
    r = head_dim // ROT_FRAC
    half = r // 2
    inv = ROPE_THETA ** (-(jnp.arange(half, dtype=F32) * 2.0 / r))
    ang = pos.astype(F32)[:, None] * inv[None, :]
    cos, sin = jnp.cos(ang), jnp.sin(ang)
    n = pos.shape[0]
    pad = jnp.zeros((n, head_dim - r), F32)
    zero = jnp.zeros((n, half), F32)
    c = jnp.concatenate([cos, cos, pad + 1.0], axis=1)
    sa = jnp.concatenate([zero, sin, pad], axis=1)
    sb = jnp.concatenate([-sin, zero, pad], axis=1)
    rep = LANE // head_dim
    return jnp.concatenate([jnp.tile(c, (1, rep)), jnp.tile(sa, (1, rep)), jnp.tile(sb, (1, rep))], axis=1)


def _lambda(lam_ref, lam_init):
    lp = lam_ref[...]
    a = jnp.sum(lp[0:1, :] * lp[1:2, :], axis=1, keepdims=True)
    b = jnp.sum(lp[2:3, :] * lp[3:4, :], axis=1, keepdims=True)
    return jnp.exp(a) - jnp.exp(b) + lam_init


def _flash_update(s, v, m, l, acc):
    mn = jnp.maximum(m, jnp.max(s, axis=1, keepdims=True))
    p = jnp.exp(s - mn)
    al = jnp.exp(m - mn)
    l = al * l + jnp.sum(p, axis=1, keepdims=True)
    acc = al * acc + _dot(p.astype(BF), v)
    return mn, l, acc


def _softplus(z):
    return jnp.maximum(z, 0.0) + jnp.log(1.0 + jnp.exp(-jnp.abs(z)))


def _suffix_matrix(n):
    return jnp.where(_iota((n, n), 0) > _iota((n, n), 1), 1.0, 0.0).astype(BF)


def _prefix_matrix(n):
    return jnp.where(_iota((n, n), 0) <= _iota((n, n), 1), 1.0, 0.0).astype(BF)


def _suffix_sum(x, mat):
    hi = x.astype(BF).astype(F32)
    both = _dot(jnp.concatenate([hi, x - hi], axis=0).astype(BF), mat)
    return both[:x.shape[0]] + both[x.shape[0]:]


def _sort_key(score):
    bits = pltpu.bitcast(score + 0.0, I32)
    return jnp.where(bits < 0, bits ^ jnp.int32(0x7FFFFFFF), bits)


def _kth_largest_key(count_ge, shape, k):
    t = jnp.full(shape, jnp.iinfo(jnp.int32).min, I32)
    for bit in range(31, -1, -1):
        step = jnp.int32(-2 ** 31) if bit == 31 else jnp.int32(1 << bit)
        cand = t + step
        t = jnp.where(count_ge(cand) >= float(k), cand, t)
    return t


def _diff_finish(acc, l, lam, gain, lam_init, rows_per_map):
    full = acc / l
    other = pltpu.roll(full, rows_per_map, 0)
    o = full - lam * other
    o = o * lax.rsqrt(jnp.mean(o * o, axis=1, keepdims=True) + RMS_EPS)
    return o * gain * (1.0 - lam_init)


def _attn_a_kernel(lam_ref, gain_ref, q_ref, k_ref, v_ref, o_ref, *, tq, tk, lam_init):
    assert tq == tk
    i = pl.program_id(1)
    lam = _lambda(lam_ref, lam_init)
    first = _iota((1, HEAD_DIM), 1) < DA_QK
    scale = DA_QK ** -0.5
    heads = [slice(h * HEAD_DIM, (h + 1) * HEAD_DIM) for h in range(H_A)]

    def chunk(c, carry, causal):
        off = pl.multiple_of(c * tk, tk)
        new = []
        for h, sl in enumerate(heads):
            q = q_ref[:, sl]
            zero = jnp.zeros_like(q)
            k = k_ref[pl.ds(off, tk), sl]
            v = v_ref[pl.ds(off, tk), sl]
            for c_map, qm in enumerate((jnp.where(first, q, zero), jnp.where(first, zero, q))):
                s = _dot_nt(qm, k) * scale
                if causal:
                    s = jnp.where(_iota((1, tk), 1) <= _iota((tq, 1), 0), s, NEG)
                new.append(_flash_update(s, v, *carry[2 * h + c_map]))
        return tuple(new)

    init = (jnp.full((tq, 1), NEG, F32), jnp.zeros((tq, 1), F32), jnp.zeros((tq, HEAD_DIM), F32))
    res = lax.fori_loop(0, i, lambda c, carry: chunk(c, carry, False), (init,) * (2 * H_A))
    res = chunk(i, res, True)
    for h, sl in enumerate(heads):
        (_, l0, a0), (_, l1, a1) = res[2 * h], res[2 * h + 1]
        o = a0 / l0 - lam * (a1 / l1)
        o = o * lax.rsqrt(jnp.mean(o * o, axis=1, keepdims=True) + RMS_EPS)
        o_ref[:, sl] = (o * gain_ref[:, sl] * (1.0 - lam_init)).astype(BF)


def prompt_attention_a(qa, ka, va, lam_params, gain, layer, lam_init, tq, tk):
    b, t, _ = qa.shape
    kern = functools.partial(_attn_a_kernel, tq=tq, tk=tk, lam_init=lam_init)
    return pl.pallas_call(
        kern,
        grid=(b, t // tq),
        in_specs=[
            pl.BlockSpec((None, 4, DA_QK), lambda bi, i: (layer, 0, 0)),
            pl.BlockSpec((None, 1, W_A), lambda bi, i: (layer, 0, 0)),
            pl.BlockSpec((None, tq, W_A), lambda bi, i: (bi, i, 0)),
            pl.BlockSpec((None, t, W_A), lambda bi, i: (bi, 0, 0)),
            pl.BlockSpec((None, t, W_A), lambda bi, i: (bi, 0, 0)),
        ],
        out_specs=pl.BlockSpec((None, tq, W_A), lambda bi, i: (bi, i, 0)),
        out_shape=jax.ShapeDtypeStruct((b, t, W_A), BF),
        compiler_params=_cparams(("parallel", "arbitrary")),
        name="prompt_attention_a",
    )(lam_params, gain, qa, ka, va)


def _stick_chunk(q, k, v, before, mat, run, acc, scale):
    z = _dot_nt(q, k) * scale
    sp = _softplus(z)
    lm = jnp.where(before, -sp, 0.0)
    tail = _suffix_sum(lm, mat)
    log_a = jnp.where(before, z - sp + tail + run, NEG)
    acc = acc + _dot(jnp.exp(log_a).astype(BF), v)
    run = run + jnp.sum(lm, axis=1, keepdims=True)
    return run, acc


def _attn_c_kernel(q_ref, k_ref, v_ref, o_ref, *, tq, tk):
    i = pl.program_id(1)
    nch = ((i + 1) * tq) // tk
    qpos = i * tq + _iota((tq, 1), 0)
    mat = _suffix_matrix(tk)
    scale = HEAD_DIM ** -0.5

    def body(it, carry):
        off = pl.multiple_of((nch - 1 - it) * tk, tk)
        before = (off + _iota((1, tk), 1)) < qpos
        new = []
        for h in range(H_C):
            sl = slice(h * HEAD_DIM, (h + 1) * HEAD_DIM)
            new.append(_stick_chunk(q_ref[:, sl], k_ref[pl.ds(off, tk), sl], v_ref[pl.ds(off, tk), sl],
                                    before, mat, *carry[h], scale))
        return tuple(new)

    init = (jnp.zeros((tq, 1), F32), jnp.zeros((tq, HEAD_DIM), F32))
    res = lax.fori_loop(0, nch, body, (init,) * H_C)
    for h in range(H_C):
        o_ref[:, h * HEAD_DIM:(h + 1) * HEAD_DIM] = res[h][1].astype(BF)


def prompt_attention_c(qc, kc, vc, tq, tk):
    b, t, _ = qc.shape
    kern = functools.partial(_attn_c_kernel, tq=tq, tk=tk)
    return pl.pallas_call(
        kern,
        grid=(b, t // tq),
        in_specs=[
            pl.BlockSpec((None, tq, W_C), lambda bi, i: (bi, i, 0)),
            pl.BlockSpec((None, t, W_C), lambda bi, i: (bi, 0, 0)),
            pl.BlockSpec((None, t, W_C), lambda bi, i: (bi, 0, 0)),
        ],
        out_specs=pl.BlockSpec((None, tq, W_C), lambda bi, i: (bi, i, 0)),
        out_shape=jax.ShapeDtypeStruct((b, t, W_C), BF),
        compiler_params=_cparams(("parallel", "arbitrary")),
        name="prompt_attention_c",
    )(qc, kc, vc)


def _indexer_scores(qi_ref, wib_sc, ki2):
    k_even = ki2[:, 0:LANE]
    k_odd = ki2[:, LANE:2 * LANE]
    nblk = ki2.shape[0] // LANE
    acc = [None] * nblk
    for hb in range(H_IDX // 2):
        qblk = qi_ref[:, hb * LANE:(hb + 1) * LANE]
        for par, kk in ((0, k_even), (1, k_odd)):
            w = wib_sc[2 * hb + par]
            d = jnp.maximum(_dot_nt(qblk, kk), 0.0)
            for b in range(nblk):
                t = d[:, b * LANE:(b + 1) * LANE] * w
                acc[b] = t if acc[b] is None else acc[b] + t
    return acc


def _select_mask(key, t, need, carry, pmat):
    gt = key > t
    eq = key == t
    eqf = jnp.where(eq, 1.0, 0.0)
    rank = _dot(eqf.astype(BF), pmat) + carry
    take = jnp.where(gt, 1.0, jnp.where(rank <= need, eqf, 0.0))
    take = jnp.where(key > KEY_NEG_INF, take, 0.0)
    return jnp.where(take > 0.5, 0.0, NEG), carry + jnp.sum(eqf, axis=1, keepdims=True)


def _attn_b_kernel(q_ref, qi_ref, wi_ref, k_ref, v_ref, ki2_ref, o_ref, wib_sc, key_sc, sel_sc, *, tq, topk):
    tk = tq
    nb = tk // LANE
    i = pl.program_id(1)
    nch = i + 1
    qpos = i * tq + _iota((tq, 1), 0)
    wi = wi_ref[...] * (D_IDX ** -0.5)
    for h in range(H_IDX):
        wib_sc[h] = jnp.broadcast_to(wi[:, h:h + 1], (tq, LANE))

    def score_body(c, _):
        off = pl.multiple_of(c * tk, tk)
        blocks = _indexer_scores(qi_ref, wib_sc, ki2_ref[pl.ds(off, tk), :])
        for b, sc in enumerate(blocks):
            vis = (off + b * LANE + _iota((1, LANE), 1)) <= qpos
            key_sc[c * nb + b] = _sort_key(jnp.where(vis, sc, -jnp.inf))
        return 0

    lax.fori_loop(0, nch, score_body, 0)

    def count(pred):
        def body(c, a):
            for b in range(nb):
                a = a + jnp.where(pred(key_sc[c * nb + b]), 1.0, 0.0)
            return a
        cnt = lax.fori_loop(0, nch, body, jnp.zeros((tq, LANE), F32))
        return jnp.sum(cnt, axis=1, keepdims=True)

    def count_ge(cand):
        candb = jnp.broadcast_to(cand, (tq, LANE))
        return count(lambda k: k >= candb)

    t = _kth_largest_key(count_ge, (tq, 1), topk)
    tb = jnp.broadcast_to(t, (tq, LANE))
    need = float(topk) - count(lambda k: k > tb)
    pmat = _prefix_matrix(LANE)

    def sel_body(c, carry):
        sel_sc[c], carry = _select_mask(key_sc[c], t, need, carry, pmat)
        return carry

    lax.fori_loop(0, nch * nb, sel_body, jnp.zeros((tq, 1), F32))

    scale = HEAD_DIM ** -0.5

    def body(c, carry):
        off = pl.multiple_of(c * tk, tk)
        mask = jnp.concatenate([sel_sc[c * nb + b] for b in range(nb)], axis=1)
        new = []
        for h in range(H_B):
            sl = slice(h * HEAD_DIM, (h + 1) * HEAD_DIM)
            s = _dot_nt(q_ref[:, sl], k_ref[pl.ds(off, tk), sl]) * scale + mask
            new.append(_flash_update(s, v_ref[pl.ds(off, tk), sl], *carry[h]))
        return tuple(new)

    init = (jnp.full((tq, 1), NEG, F32), jnp.zeros((tq, 1), F32), jnp.zeros((tq, HEAD_DIM), F32))
    res = lax.fori_loop(0, nch, body, (init,) * H_B)
    for h in range(H_B):
        _, l, acc = res[h]
        o_ref[:, h * HEAD_DIM:(h + 1) * HEAD_DIM] = (acc / l).astype(BF)


def prompt_attention_b(qb, qi, wi, kb, vb, ki2, topk, tq):
    b, t, _ = qb.shape
    kern = functools.partial(_attn_b_kernel, tq=tq, topk=topk)
    qspec = lambda w: pl.BlockSpec((None, tq, w), lambda bi, i: (bi, i, 0))
    kspec = lambda w: pl.BlockSpec((None, t, w), lambda bi, i: (bi, 0, 0))
    return pl.pallas_call(
        kern,
        grid=(b, t // tq),
        in_specs=[qspec(W_B), qspec(W_QI), qspec(LANE), kspec(W_B), kspec(W_B), kspec(2 * LANE)],
        out_specs=qspec(W_B),
        out_shape=jax.ShapeDtypeStruct((b, t, W_B), BF),
        scratch_shapes=[pltpu.VMEM((H_IDX, tq, LANE), F32),
                        pltpu.VMEM((t // LANE, tq, LANE), I32), pltpu.VMEM((t // LANE, tq, LANE), F32)],
        compiler_params=_cparams(("parallel", "arbitrary")),
        name="prompt_attention_b",
    )(qb, qi, wi, kb, vb, ki2)


ROWS = 8


def _page_heads(page_refs, kv, h, heads):
    return [r[pl.ds(2 * h + kv, LANE, stride=2 * heads), :].astype(BF) for r in page_refs]


def flat_page_view(cache):
    depth, pool, slots, _, heads, hd = cache.shape
    return jnp.transpose(cache, (0, 1, 2, 4, 3, 5)).reshape(depth, pool, slots * heads * 2, hd)


def _paged_softmax_kernel(pt_ref, *refs, n_groups, group, heads, kind, lam_init, t_new):
    del pt_ref
    if kind == "a":
        lam_ref, gain_ref, q_ref, kn_ref, vn_ref = refs[:5]
        rest = refs[5:]
        mask_ref = None
    else:
        q_ref, kn_ref, vn_ref, mask_ref = refs[:4]
        rest = refs[4:]
    page_refs = rest[:group]
    o_ref, m_sc, l_sc, acc_sc = rest[group:]
    j = pl.program_id(1)
    scale = (DA_QK if kind == "a" else HEAD_DIM) ** -0.5

    @pl.when(j == 0)
    def _():
        m_sc[...] = jnp.full(m_sc.shape, NEG, F32)
        l_sc[...] = jnp.zeros(l_sc.shape, F32)
        acc_sc[...] = jnp.zeros(acc_sc.shape, F32)

    state = [(m_sc[h][:, 0:1], l_sc[h][:, 0:1], acc_sc[h]) for h in range(heads)]

    @pl.when(j < n_groups)
    def _():
        new = []
        for h in range(heads):
            q = q_ref[h]
            ks = _page_heads(page_refs, 0, h, heads)
            vs = _page_heads(page_refs, 1, h, heads)
            s = jnp.concatenate([_dot_nt(q, k) for k in ks], axis=1) * scale
            if mask_ref is not None:
                s = s + jnp.concatenate([mask_ref[g] for g in range(group)], axis=1)
            new.append(_flash_update(s, jnp.concatenate(vs, axis=0), *state[h]))
        for h, (m, l, acc) in enumerate(new):
            m_sc[h] = jnp.broadcast_to(m, (ROWS, LANE))
            l_sc[h] = jnp.broadcast_to(l, (ROWS, LANE))
            acc_sc[h] = acc

    @pl.when(j == n_groups)
    def _():
        if kind == "a":
            lam = _lambda(lam_ref, lam_init)
            row = _iota((ROWS, LANE), 0)
            ok = _iota((ROWS, LANE), 1) <= jnp.where(row >= t_new, row - t_new, row)
            new_mask = jnp.where(ok, 0.0, NEG)
        else:
            new_mask = mask_ref[0]
        for h in range(heads):
            sl = slice(h * HEAD_DIM, (h + 1) * HEAD_DIM)
            s = _dot_nt(q_ref[h], kn_ref[:, sl]) * scale + new_mask
            _, l, acc = _flash_update(s, vn_ref[:, sl], *state[h])
            if kind == "a":
                o_ref[h] = _diff_finish(acc, l, lam, gain_ref[:, sl], lam_init, t_new)
            else:
                o_ref[h] = acc / l


def sample_paged_softmax(kind, page_table, cache, layer, q, k_new, v_new, *, mask=None,
                         lam_params=None, gain=None, lam_init=0.0, t_new=4, group=4):
    bsz, heads = q.shape[0], q.shape[1]
    n_pages = page_table.shape[1]
    n_groups = n_pages // group
    hw = heads * HEAD_DIM
    kern = functools.partial(_paged_softmax_kernel, n_groups=n_groups, group=group, heads=heads,
                             kind=kind, lam_init=lam_init, t_new=t_new)

    def page_spec(g):
        return pl.BlockSpec(
            (None, None, LANE * 2 * heads, HEAD_DIM),
            lambda b, j, pt: (layer, pt[b, jnp.minimum(j * group + g, n_pages - 1)], 0, 0))

    in_specs = []
    args = []
    if kind == "a":
        in_specs += [pl.BlockSpec((None, 4, DA_QK), lambda b, j, pt: (layer, 0, 0)),
                     pl.BlockSpec((None, 1, hw), lambda b, j, pt: (layer, 0, 0))]
        args += [lam_params, gain]
    in_specs += [pl.BlockSpec((None, heads, ROWS, HEAD_DIM), lambda b, j, pt: (b, 0, 0, 0)),
                 pl.BlockSpec((None, LANE, hw), lambda b, j, pt: (b, 0, 0)),
                 pl.BlockSpec((None, LANE, hw), lambda b, j, pt: (b, 0, 0))]
    args += [q, k_new, v_new]
    if kind != "a":
        in_specs.append(pl.BlockSpec((None, group, ROWS, LANE), lambda b, j, pt: (b, j, 0, 0)))
        args.append(mask)
    in_specs += [page_spec(g) for g in range(group)]
    args += [cache] * group
    return pl.pallas_call(
        kern,
        grid_spec=pltpu.PrefetchScalarGridSpec(
            num_scalar_prefetch=1,
            grid=(bsz, n_groups + 1),
            in_specs=in_specs,
            out_specs=pl.BlockSpec((None, heads, ROWS, HEAD_DIM), lambda b, j, pt: (b, 0, 0, 0)),
            scratch_shapes=[pltpu.VMEM((heads, ROWS, LANE), F32)] * 3,
        ),
        out_shape=jax.ShapeDtypeStruct((bsz, heads, ROWS, HEAD_DIM), F32),
        compiler_params=_cparams(("parallel", "arbitrary")),
        name="sample_paged_softmax_" + kind,
    )(page_table, *args)


def _paged_stick_kernel(pt_ref, q_ref, kn_ref, vn_ref, *rest, group, heads, t_new):
    del pt_ref
    page_refs = rest[:group]
    o_ref, run_sc, acc_sc = rest[group:]
    j = pl.program_id(1)
    scale = HEAD_DIM ** -0.5
    mat = _suffix_matrix(LANE)

    def store(new):
        for h, (run, acc) in enumerate(new):
            run_sc[h] = jnp.broadcast_to(run, (ROWS, LANE))
            acc_sc[h] = acc

    @pl.when(j == 0)
    def _():
        before = _iota((ROWS, LANE), 1) < jnp.minimum(_iota((ROWS, LANE), 0), t_new)
        zero = (jnp.zeros((ROWS, 1), F32), jnp.zeros((ROWS, HEAD_DIM), F32))
        store([_stick_chunk(q_ref[h], kn_ref[:, h * HEAD_DIM:(h + 1) * HEAD_DIM],
                            vn_ref[:, h * HEAD_DIM:(h + 1) * HEAD_DIM], before, mat, *zero, scale)
               for h in range(heads)])

    @pl.when(j > 0)
    def _():
        state = [(run_sc[h][:, 0:1], acc_sc[h]) for h in range(heads)]
        new = []
        for h in range(heads):
            q = q_ref[h]
            ks = _page_heads(page_refs, 0, h, heads)
            vs = _page_heads(page_refs, 1, h, heads)
            z = jnp.concatenate([_dot_nt(q, k) for k in ks], axis=0) * scale
            sp = _softplus(z)
            log_a = z - sp + _suffix_sum(-sp, mat)
            tot = jnp.sum(-sp, axis=1, keepdims=True)
            run, acc = state[h]
            for g in range(group):
                rows = slice(g * ROWS, (g + 1) * ROWS)
                acc = acc + _dot(jnp.exp(log_a[rows] + run).astype(BF), vs[g])
                run = run + tot[rows]
            new.append((run, acc))
        store(new)

    @pl.when(j == pl.num_programs(1) - 1)
    def _():
        o_ref[...] = acc_sc[...]


def sample_paged_stick(page_table, cache, layer, q, k_new, v_new, *, t_new=4, group=4):
    bsz, heads = q.shape[0], q.shape[1]
    n_pages = page_table.shape[1]
    n_groups = n_pages // group
    hw = heads * HEAD_DIM
    kern = functools.partial(_paged_stick_kernel, group=group, heads=heads, t_new=t_new)

    def page_spec(g):
        return pl.BlockSpec(
            (None, None, LANE * 2 * heads, HEAD_DIM),
            lambda b, j, pt: (layer, pt[b, n_pages - 1 - (jnp.maximum(j - 1, 0) * group + g)], 0, 0))

    in_specs = [pl.BlockSpec((None, heads, ROWS, HEAD_DIM), lambda b, j, pt: (b, 0, 0, 0)),
                pl.BlockSpec((None, LANE, hw), lambda b, j, pt: (b, 0, 0)),
                pl.BlockSpec((None, LANE, hw), lambda b, j, pt: (b, 0, 0))]
    in_specs += [page_spec(g) for g in range(group)]
    return pl.pallas_call(
        kern,
        grid_spec=pltpu.PrefetchScalarGridSpec(
            num_scalar_prefetch=1,
            grid=(bsz, n_groups + 1),
            in_specs=in_specs,
            out_specs=pl.BlockSpec((None, heads, ROWS, HEAD_DIM), lambda b, j, pt: (b, 0, 0, 0)),
            scratch_shapes=[pltpu.VMEM((heads, ROWS, LANE), F32)] * 2,
        ),
        out_shape=jax.ShapeDtypeStruct((bsz, heads, ROWS, HEAD_DIM), F32),
        compiler_params=_cparams(("parallel", "arbitrary")),
        name="sample_paged_stick",
    )(page_table, q, k_new, v_new, *([cache] * group))


def _sample_score_kernel(pt_ref, qi_ref, wi_ref, kin_ref, *rest, n_groups, group, t_new):
    del pt_ref
    page_refs = rest[:group]
    o_ref = rest[group]
    j = pl.program_id(1)
    qi = qi_ref[...]
    wi = wi_ref[...]

    def score(dots):
        d = jnp.maximum(dots * (D_IDX ** -0.5), 0.0) * wi
        return jnp.sum(d.reshape(H_IDX, ROWS, LANE), axis=0)

    @pl.when(j < n_groups)
    def _():
        for g in range(group):
            o_ref[g] = score(_dot(qi, page_refs[g][...].astype(BF)))

    @pl.when(j == n_groups)
    def _():
        ok = (_iota((ROWS, LANE), 1) <= _iota((ROWS, LANE), 0)) & (_iota((ROWS, LANE), 1) < t_new)
        o_ref[0] = jnp.where(ok, score(_dot_nt(qi, kin_ref[...])), -jnp.inf)
        for g in range(1, group):
            o_ref[g] = jnp.full((ROWS, LANE), -jnp.inf, F32)


def sample_indexer_scores(page_table, cache_idx, layer, qi, wi, ki_new, *, t_new=4, group=4):
    bsz = qi.shape[0]
    n_pages = page_table.shape[1]
    n_groups = n_pages // group
    kern = functools.partial(_sample_score_kernel, n_groups=n_groups, group=group, t_new=t_new)

    def page_spec(g):
        return pl.BlockSpec(
            (None, None, D_IDX, LANE),
            lambda b, j, pt: (layer, pt[b, jnp.minimum(j * group + g, n_pages - 1)], 0, 0))

    in_specs = [pl.BlockSpec((None, H_IDX * ROWS, D_IDX), lambda b, j, pt: (b, 0, 0)),
                pl.BlockSpec((None, H_IDX * ROWS, 1), lambda b, j, pt: (b, 0, 0)),
                pl.BlockSpec((None, LANE, D_IDX), lambda b, j, pt: (b, 0, 0))]
    in_specs += [page_spec(g) for g in range(group)]
    return pl.pallas_call(
        kern,
        grid_spec=pltpu.PrefetchScalarGridSpec(
            num_scalar_prefetch=1,
            grid=(bsz, n_groups + 1),
            in_specs=in_specs,
            out_specs=pl.BlockSpec((None, group, ROWS, LANE), lambda b, j, pt: (b, j, 0, 0)),
        ),
        out_shape=jax.ShapeDtypeStruct((bsz, (n_groups + 1) * group, ROWS, LANE), F32),
        compiler_params=_cparams(("parallel", "arbitrary")),
        name="sample_indexer_scores",
    )(page_table, qi, wi, ki_new, *([cache_idx] * group))


def _sample_select_kernel(s_ref, o_ref, key_sc, *, topk):
    nch = s_ref.shape[0]

    def key_body(c, _):
        key_sc[c] = _sort_key(s_ref[c])
        return 0

    lax.fori_loop(0, nch, key_body, 0)

    unroll = 4 if nch % 4 == 0 else 1

    def count(pred):
        def body(c, a):
            for u in range(unroll):
                a = a + jnp.where(pred(key_sc[c * unroll + u]), 1.0, 0.0)
            return a
        cnt = lax.fori_loop(0, nch // unroll, body, jnp.zeros((ROWS, LANE), F32))
        return jnp.sum(cnt, axis=1, keepdims=True)

    def count_ge(cand):
        candb = jnp.broadcast_to(cand, (ROWS, LANE))
        return count(lambda k: k >= candb)

    t = _kth_largest_key(count_ge, (ROWS, 1), topk)
    tb = jnp.broadcast_to(t, (ROWS, LANE))
    need = float(topk) - count(lambda k: k > tb)
    pmat = _prefix_matrix(LANE)

    def sel_body(c, carry):
        o_ref[c], carry = _select_mask(key_sc[c], t, need, carry, pmat)
        return carry

    lax.fori_loop(0, nch, sel_body, jnp.zeros((ROWS, 1), F32))


def sample_select_mask(scores, topk):
    bsz, nch = scores.shape[:2]
    blk = pl.BlockSpec((None, nch, ROWS, LANE), lambda b: (b, 0, 0, 0))
    return pl.pallas_call(
        functools.partial(_sample_select_kernel, topk=topk),
        grid=(bsz,),
        in_specs=[blk],
        out_specs=blk,
        out_shape=jax.ShapeDtypeStruct(scores.shape, F32),
        scratch_shapes=[pltpu.VMEM((nch, ROWS, LANE), I32)],
        compiler_params=_cparams(("parallel",)),
        name="sample_select_mask",
    )(scores)


def _layer_norm(x, g, b):
    mu = jnp.mean(x, axis=1, keepdims=True)
    cen = x - mu
    var = jnp.mean(cen * cen, axis=1, keepdims=True)
    return cen * lax.rsqrt(var + LN_EPS) * g + b


def _out_kernel(oa_ref, ob_ref, oc_ref, w_ref, x_ref, ga_ref, lng_ref, lnb_ref, sc_ref, sh_ref,
                wr_ref, br_ref, x1_ref, h_ref, lg_ref, *, alpha):
    attn = (_dot(oa_ref[...], w_ref[0:W_A, :])
            + _dot(ob_ref[...], w_ref[W_A:W_A + W_B, :])
            + _dot(oc_ref[...], w_ref[W_A + W_B:W_A + W_B + W_C, :]))
    x1 = _layer_norm(alpha * x_ref[...] + (1.0 + ga_ref[...]) * attn, lng_ref[...], lnb_ref[...])
    x1_ref[...] = x1
    h = x1 * (1.0 + sc_ref[...]) + sh_ref[...]
    h_ref[...] = h
    hh = h.astype(BF)
    hl = (h - hh.astype(F32)).astype(BF)
    w = wr_ref[...]
    wh = w.astype(BF)
    wl = (w - wh.astype(F32)).astype(BF)
    lg_ref[...] = _dot(hh, wh) + _dot(hh, wl) + _dot(hl, wh) + br_ref[...]


def out_projection(oa, ob, oc, w_out, x, g_a, ln_g, ln_b, sc_f, sh_f, w_router, b_router, layer, alpha, tm):
    b, t, d = x.shape
    ts = g_a.shape[1]
    tms = 1 if ts == 1 else tm
    mod_map = (lambda bi, i: (bi, 0, 0)) if ts == 1 else (lambda bi, i: (bi, i, 0))
    ne = w_router.shape[-1]
    row = lambda w: pl.BlockSpec((None, tm, w), lambda bi, i: (bi, i, 0))
    mod = pl.BlockSpec((None, tms, d), mod_map)
    par = lambda w: pl.BlockSpec((None, 1, w), lambda bi, i: (layer, 0, 0))
    return pl.pallas_call(
        functools.partial(_out_kernel, alpha=alpha),
        grid=(b, t // tm),
        in_specs=[row(W_A), row(W_B), row(W_C),
                  pl.BlockSpec((None, w_out.shape[1], d), lambda bi, i: (layer, 0, 0)),
                  row(d), mod, par(d), par(d), mod, mod,
                  pl.BlockSpec((None, d, ne), lambda bi, i: (layer, 0, 0)), par(ne)],
        out_specs=[row(d), row(d), row(ne)],
        out_shape=[jax.ShapeDtypeStruct((b, t, d), F32), jax.ShapeDtypeStruct((b, t, d), F32),
                   jax.ShapeDtypeStruct((b, t, ne), F32)],
        compiler_params=_cparams(("parallel", "parallel")),
        name="out_projection",
    )(oa, ob, oc, w_out, x, g_a, ln_g, ln_b, sc_f, sh_f, w_router, b_router)


def _router_kernel(lg_ref, idx_ref, w_ref):
    lg = lg_ref[...]
    n, ne = lg.shape
    lane = _iota((n, ne), 1).astype(F32)
    vals, idxs = [], []
    for _ in range(TOP_K_EXPERTS):
        mx = jnp.max(lg, axis=1, keepdims=True)
        ix = jnp.min(jnp.where(lg == mx, lane, float(ne)), axis=1, keepdims=True)
        vals.append(mx)
        idxs.append(ix)
        lg = jnp.where(lane == ix, -jnp.inf, lg)
    es = [jnp.exp(v - vals[0]) for v in vals]
    tot = es[0]
    for e in es[1:]:
        tot = tot + e
    out_lane = _iota((n, TOP_K_EXPERTS), 1)
    idx = jnp.zeros((n, TOP_K_EXPERTS), F32)
    wgt = jnp.zeros((n, TOP_K_EXPERTS), F32)
    for k in range(TOP_K_EXPERTS):
        idx = jnp.where(out_lane == k, idxs[k], idx)
        wgt = jnp.where(out_lane == k, es[k] / tot, wgt)
    idx_ref[...] = idx.astype(I32)
    w_ref[...] = wgt


def router_topk(logits, tm):
    n, ne = logits.shape
    return pl.pallas_call(
        _router_kernel,
        grid=(n // tm,),
        in_specs=[pl.BlockSpec((tm, ne), lambda i: (i, 0))],
        out_specs=[pl.BlockSpec((tm, TOP_K_EXPERTS), lambda i: (i, 0))] * 2,
        out_shape=[jax.ShapeDtypeStruct((n, TOP_K_EXPERTS), I32), jax.ShapeDtypeStruct((n, TOP_K_EXPERTS), F32)],
        compiler_params=_cparams(("parallel",)),
        name="router_topk",
    )(logits)


def _gate_up_kernel(te_ref, tv_ref, x_ref, wg_ref, wu_ref, bg_ref, bu_ref, o_ref, wg_sc, wu_sc):
    i = pl.program_id(1)
    changed = jnp.logical_or(i == 0, te_ref[i] != te_ref[jnp.maximum(i - 1, 0)])

    @pl.when(changed)
    def _():
        wg_sc[...] = wg_ref[...].astype(BF)
        wu_sc[...] = wu_ref[...].astype(BF)

    @pl.when(tv_ref[i] > 0)
    def _():
        x = x_ref[...].astype(BF)
        g = jnp.minimum(_dot(x, wg_sc[...]) + bg_ref[...], SWIGLU_LIMIT)
        u = jnp.clip(_dot(x, wu_sc[...]) + bu_ref[...], -SWIGLU_LIMIT, SWIGLU_LIMIT)
        o_ref[...] = ((u + 1.0) * g * jax.nn.sigmoid(SWIGLU_ALPHA * g)).astype(BF)

    @pl.when(tv_ref[i] == 0)
    def _():
        o_ref[...] = jnp.zeros(o_ref.shape, BF)


def expert_gate_up(tile_expert, tile_valid, xs, w_gate_up, b_gate_up, layer, tm, tn):
    r, d = xs.shape
    dff = w_gate_up.shape[-1] // 2
    nj = dff // tn
    ne = w_gate_up.shape[1]
    bias = b_gate_up.reshape(b_gate_up.shape[0], ne, 1, 2 * dff)
    return pl.pallas_call(
        _gate_up_kernel,
        grid_spec=pltpu.PrefetchScalarGridSpec(
            num_scalar_prefetch=2,
            grid=(nj, r // tm),
            in_specs=[
                pl.BlockSpec((tm, d), lambda j, i, te, tv: (i, 0)),
                pl.BlockSpec((None, None, d, tn), lambda j, i, te, tv: (layer, te[i], 0, j)),
                pl.BlockSpec((None, None, d, tn), lambda j, i, te, tv: (layer, te[i], 0, j + nj)),
                pl.BlockSpec((None, None, 1, tn), lambda j, i, te, tv: (layer, te[i], 0, j)),
                pl.BlockSpec((None, None, 1, tn), lambda j, i, te, tv: (layer, te[i], 0, j + nj)),
            ],
            out_specs=pl.BlockSpec((tm, tn), lambda j, i, te, tv: (i, j)),
            scratch_shapes=[pltpu.VMEM((d, tn), BF)] * 2,
        ),
        out_shape=jax.ShapeDtypeStruct((r, dff), BF),
        compiler_params=_cparams(("arbitrary", "arbitrary")),
        name="expert_gate_up",
    )(tile_expert, tile_valid, xs, w_gate_up, w_gate_up, bias, bias)


def _down_kernel(te_ref, tv_ref, a_ref, w_ref, b_ref, o_ref, w_sc):
    i = pl.program_id(1)
    changed = jnp.logical_or(i == 0, te_ref[i] != te_ref[jnp.maximum(i - 1, 0)])

    @pl.when(changed)
    def _():
        w_sc[...] = w_ref[...].astype(BF)

    @pl.when(tv_ref[i] > 0)
    def _():
        o_ref[...] = _dot(a_ref[...], w_sc[...]) + b_ref[...]

    @pl.when(tv_ref[i] == 0)
    def _():
        o_ref[...] = jnp.zeros(o_ref.shape, F32)


def expert_down(tile_expert, tile_valid, act, w_down, b_down, layer, tm, tn):
    r, dff = act.shape
    d = w_down.shape[-1]
    ne = w_down.shape[1]
    bias = b_down.reshape(b_down.shape[0], ne, 1, d)
    return pl.pallas_call(
        _down_kernel,
        grid_spec=pltpu.PrefetchScalarGridSpec(
            num_scalar_prefetch=2,
            grid=(d // tn, r // tm),
            in_specs=[
                pl.BlockSpec((tm, dff), lambda j, i, te, tv: (i, 0)),
                pl.BlockSpec((None, None, dff, tn), lambda j, i, te, tv: (layer, te[i], 0, j)),
                pl.BlockSpec((None, None, 1, tn), lambda j, i, te, tv: (layer, te[i], 0, j)),
            ],
            out_specs=pl.BlockSpec((tm, tn), lambda j, i, te, tv: (i, j)),
            scratch_shapes=[pltpu.VMEM((dff, tn), BF)],
        ),
        out_shape=jax.ShapeDtypeStruct((r, d), F32),
        compiler_params=_cparams(("arbitrary", "arbitrary")),
        name="expert_down",
    )(tile_expert, tile_valid, act, w_down, bias)


def _ffn_norm_kernel(x_ref, y_ref, w_ref, g_ref, lng_ref, lnb_ref, o_ref, *, alpha):
    w = w_ref[...]
    ffn = y_ref[0] * w[:, 0:1]
    for k in range(1, TOP_K_EXPERTS):
        ffn = ffn + y_ref[k] * w[:, k:k + 1]
    o_ref[...] = _layer_norm(alpha * x_ref[...] + (1.0 + g_ref[...]) * ffn, lng_ref[...], lnb_ref[...])


def ffn_combine_norm(x1, y4, top_w, g_f, ln_g, ln_b, layer, alpha, tm):
    b, t, d = x1.shape
    ts = g_f.shape[1]
    tms = 1 if ts == 1 else tm
    mod_map = (lambda bi, i: (bi, 0, 0)) if ts == 1 else (lambda bi, i: (bi, i, 0))
    row = pl.BlockSpec((None, tm, d), lambda bi, i: (bi, i, 0))
    par = pl.BlockSpec((None, 1, d), lambda bi, i: (layer, 0, 0))
    return pl.pallas_call(
        functools.partial(_ffn_norm_kernel, alpha=alpha),
        grid=(b, t // tm),
        in_specs=[row, pl.BlockSpec((TOP_K_EXPERTS, None, tm, d), lambda bi, i: (0, bi, i, 0)),
                  pl.BlockSpec((None, tm, TOP_K_EXPERTS), lambda bi, i: (bi, i, 0)),
                  pl.BlockSpec((None, tms, d), mod_map), par, par],
        out_specs=row,
        out_shape=jax.ShapeDtypeStruct((b, t, d), F32),
        compiler_params=_cparams(("parallel", "parallel")),
        name="ffn_combine_norm",
    )(x1, y4, top_w, g_f, ln_g, ln_b)


def _route(top_idx, n_experts, tm):
    n, k = top_idx.shape
    flat = top_idx.reshape(-1)
    onehot = (flat[:, None] == jnp.arange(n_experts, dtype=I32)[None, :]).astype(I32)
    csum = jnp.cumsum(onehot, axis=0)
    sizes = csum[-1]
    padded = ((sizes + tm - 1) // tm) * tm
    pend = jnp.cumsum(padded)
    pstarts = pend - padded
    n_rows = ((n * k + n_experts * (tm - 1)) // tm + 1) * tm
    dest_flat = jnp.sum(onehot * (csum - 1 + pstarts[None, :]), axis=1)
    src_token = jnp.zeros((n_rows,), I32).at[dest_flat].set(jnp.arange(n * k, dtype=I32) // k)
    tile_start = jnp.arange(n_rows // tm, dtype=I32) * tm
    te = jnp.minimum(jnp.sum((pend[None, :] <= tile_start[:, None]).astype(I32), axis=1), n_experts - 1)
    tv = (tile_start < pend[-1]).astype(I32)
    return src_token, dest_flat.reshape(n, k), te, tv


def moe_experts(h_tok, top_idx, w_gate_up, b_gate_up, w_down, b_down, layer, tm_route):
    d = h_tok.shape[1]
    ne = w_gate_up.shape[1]
    dff = w_down.shape[2]
    src_token, dest, te, tv = _route(top_idx, ne, tm_route)
    xs = jnp.take(h_tok, src_token, axis=0)
    act = expert_gate_up(te, tv, xs, w_gate_up, b_gate_up, layer, tm_route, min(512, dff))
    return expert_down(te, tv, act, w_down, b_down, layer, tm_route, min(512, d)), dest


def _pad_w_in(w_in):
    depth, d, _ = w_in.shape
    z = lambda n: jnp.zeros((depth, d, n), w_in.dtype)
    return jnp.concatenate([
        w_in[:, :, :S_WI], w_in[:, :, S_WI:S_KI], z(LANE - H_IDX),
        w_in[:, :, S_KI:S_QC], z(LANE - D_IDX), w_in[:, :, S_QC:]], axis=2).astype(BF)


def _pick(n, prefs):
    for p in prefs:
        if n % p == 0:
            return p
    return n


def kernel(x_prompt, x_sample, cache_kv_a, cache_kv_b, cache_idx_b, cache_kv_c, page_table, c_prompt, c_sample,
           w_ada, b_ada, w_in, ln_idx_g, ln_idx_b, lam_params, gn_a_gain, w_out, ln1_g, ln1_b, w_router,
           b_router, w_gate_up, b_gate_up, w_down, b_down, ln2_g, ln2_b):
    depth = w_in.shape[0]
    bp, tp, d = x_prompt.shape
    bs, t_new, _ = x_sample.shape
    n_pages = page_table.shape[1]
    past_len = n_pages * cache_kv_a.shape[2]
    alpha = (2 * depth) ** 0.25
    topk_p = min(TOPK_MAX, tp // 4)
    topk_s = min(TOPK_MAX, (past_len + t_new) // 4)
    ne = w_router.shape[-1]
    group = _pick(n_pages, (8, 4, 2, 1))

    w_in_p = _pad_w_in(w_in)
    w_out_b = w_out.astype(BF)
    pad_lane = lambda v: jnp.pad(v, ((0, 0), (0, LANE - v.shape[1])))
    ln_gb = jnp.stack([pad_lane(ln_idx_g), pad_lane(ln_idx_b)], axis=1)
    gain = gn_a_gain.reshape(depth, 1, W_A)
    vec = lambda v: v.reshape(depth, 1, -1)

    pos_p = jnp.arange(tp, dtype=I32)
    pos_s = past_len + jnp.arange(t_new, dtype=I32)
    tab64_p = jnp.tile(rope_tables(pos_p, DA_QK), (bp, 1))
    tab128_p = jnp.tile(rope_tables(pos_p, HEAD_DIM), (bp, 1))
    tab64_s = jnp.tile(rope_tables(pos_s, DA_QK), (bs, 1))
    tab128_s = jnp.tile(rope_tables(pos_s, HEAD_DIM), (bs, 1))

    n_c = bp + bs
    rows_c = -(-n_c // 8) * 8
    c_all = jnp.pad(jnp.concatenate([c_prompt, c_sample], axis=0), ((0, rows_c - n_c), (0, 0)))
    mod = ada_modulation(c_all, w_ada, b_ada, tn=_pick(6 * d, (1024, 512, 256, 128))).reshape(depth, rows_c, 6, d)

    tm_proj = _pick(tp, (1024, 512, 256, 128))
    tm_epi = _pick(tp, (256, 128))
    tq_a = _pick(tp, (256, 128))
    tm_out = _pick(tp, (256, 128))
    ms = bs * t_new
    kva_pages, kvb_pages, kvc_pages = [flat_page_view(c) for c in (cache_kv_a, cache_kv_b, cache_kv_c)]
    idx_pages = jnp.swapaxes(cache_idx_b, 2, 3)

    xp, xs = x_prompt, x_sample.reshape(1, ms, d)
    outs_p, outs_s = [], []
    first = jnp.arange(HEAD_DIM) < DA_QK
    for l in range(depth):
        lam_init = 0.8 - 0.6 * math.exp(-0.3 * l)
        mp = mod[l, :bp][:, :, None, :]
        msr = jnp.repeat(mod[l, bp:n_c], t_new, axis=0)[None]
        sh_a, sc_a, g_a, sh_f, sc_f, g_f = [mp[:, i] for i in range(6)]
        sh_as, sc_as, g_as, sh_fs, sc_fs, g_fs = [msr[:, :, i] for i in range(6)]

        proj = modulated_projection(xp, sc_a, sh_a, w_in_p, l, tm_proj).reshape(bp * tp, N_PAD)
        (kv_a, kv_b, idx, kv_c, qa, ka, va, qb, kb, vb, qi, wi, ki2, qc, kc, vc) = projection_epilogue(
            proj, tab64_p, tab128_p, ln_gb[l], tm_epi)
        r3 = lambda a: a.reshape(bp, tp, a.shape[-1])
        oa = prompt_attention_a(r3(qa), r3(ka), r3(va), lam_params, gain, l, lam_init, tq_a, min(tq_a, 256))
        ob = prompt_attention_b(r3(qb), r3(qi), r3(wi), r3(kb), r3(vb), r3(ki2), topk_p, tq_a)
        oc = prompt_attention_c(r3(qc), r3(kc), r3(vc), tq_a, LANE)
        x1p, hp, lgp = out_projection(oa, ob, oc, w_out_b, xp, g_a, vec(ln1_g), vec(ln1_b), sc_f, sh_f,
                                      w_router, vec(b_router), l, alpha, tm_out)
        outs_p.append((kv_a.reshape(bp, tp, 2, H_A, HEAD_DIM), kv_b.reshape(bp, tp, 2, H_B, HEAD_DIM),
                       idx.reshape(bp, tp, D_IDX), kv_c.reshape(bp, tp, 2, H_C, HEAD_DIM)))

        proj_s = modulated_projection(xs, sc_as, sh_as, w_in_p, l, ms).reshape(ms, N_PAD)
        (kv_a_s, kv_b_s, idx_s, kv_c_s, qa_s, ka_s, va_s, qb_s, kb_s, vb_s, qi_s, wi_s, ki2_s, qc_s, kc_s,
         vc_s) = projection_epilogue(proj_s, tab64_s, tab128_s, ln_gb[l], ms)
        new_rows = lambda a: jnp.pad(a.reshape(bs, t_new, a.shape[-1]), ((0, 0), (0, LANE - t_new), (0, 0)))
        heads_first = lambda a, h: jnp.swapaxes(a.reshape(bs, t_new, h, HEAD_DIM), 1, 2)
        pad_rows = lambda a: jnp.pad(a, ((0, 0), (0, 0), (0, ROWS - t_new), (0, 0)))
        qa_h = heads_first(qa_s, H_A)
        qa_2 = jnp.concatenate([jnp.where(first, qa_h, 0), jnp.where(first, 0, qa_h)], axis=2).astype(BF)
        oa_s = sample_paged_softmax("a", page_table, kva_pages, l, qa_2, new_rows(ka_s), new_rows(va_s),
                                    lam_params=lam_params, gain=gain, lam_init=lam_init, t_new=t_new, group=group)
        qi_h = pad_rows(jnp.swapaxes(qi_s.reshape(bs, t_new, H_IDX, D_IDX), 1, 2)).reshape(bs, H_IDX * ROWS, D_IDX)
        wi_h = pad_rows(jnp.swapaxes(wi_s[:, :H_IDX].reshape(bs, t_new, H_IDX, 1), 1, 2)).reshape(
            bs, H_IDX * ROWS, 1)
        scores = sample_indexer_scores(page_table, idx_pages, l, qi_h, wi_h, new_rows(ki2_s[:, :D_IDX]),
                                       t_new=t_new, group=group)
        sel = sample_select_mask(scores, topk_s)
        ob_s = sample_paged_softmax("b", page_table, kvb_pages, l, pad_rows(heads_first(qb_s, H_B)),
                                    new_rows(kb_s), new_rows(vb_s), mask=sel, t_new=t_new, group=group)
        oc_s = sample_paged_stick(page_table, kvc_pages, l, pad_rows(heads_first(qc_s, H_C)),
                                  new_rows(kc_s), new_rows(vc_s), t_new=t_new, group=group)
        tok = lambda o, h: jnp.swapaxes(o[:, :, :t_new], 1, 2).reshape(1, ms, h * HEAD_DIM).astype(BF)
        x1s, hs, lgs = out_projection(tok(oa_s, H_A), tok(ob_s, H_B), tok(oc_s, H_C), w_out_b, xs, g_as,
                                      vec(ln1_g), vec(ln1_b), sc_fs, sh_fs, w_router, vec(b_router), l, alpha, ms)
        outs_s.append((kv_a_s.reshape(bs, t_new, 2, H_A, HEAD_DIM), kv_b_s.reshape(bs, t_new, 2, H_B, HEAD_DIM),
                       idx_s.reshape(bs, t_new, D_IDX), kv_c_s.reshape(bs, t_new, 2, H_C, HEAD_DIM)))

        np_ = bp * tp
        idx_p, w_p = router_topk(lgp.reshape(np_, ne), tm_out)
        idx_s, w_s = router_topk(lgs.reshape(ms, ne), ms)
        h_tok = jnp.concatenate([hp.reshape(np_, d), hs.reshape(ms, d)], axis=0)
        y, dest = moe_experts(h_tok, jnp.concatenate([idx_p, idx_s], axis=0), w_gate_up, b_gate_up,
                              w_down, b_down, l, 256)
        y4p = jnp.take(y, dest[:np_].T.reshape(TOP_K_EXPERTS, bp, tp), axis=0)
        y4s = jnp.take(y, dest[np_:].T.reshape(TOP_K_EXPERTS, 1, ms), axis=0)
        xp = ffn_combine_norm(x1p, y4p, w_p.reshape(bp, tp, TOP_K_EXPERTS), g_f, vec(ln2_g), vec(ln2_b),
                              l, alpha, tm_out)
        xs = ffn_combine_norm(x1s, y4s, w_s.reshape(1, ms, TOP_K_EXPERTS), g_fs, vec(ln2_g), vec(ln2_b),
                              l, alpha, ms)

    stack = lambda outs, i: jnp.stack([o[i] for o in outs])
    return (xp, xs.reshape(bs, t_new, d),
            stack(outs_p, 0), stack(outs_p, 1), stack(outs_p, 2), stack(outs_p, 3),
            stack(outs_s, 0), stack(outs_s, 1), stack(outs_s, 2), stack(outs_s, 3))
```

```python
import functools
import math

import jax
import jax.numpy as jnp
from jax import lax
from jax.experimental import pallas as pl
from jax.experimental.pallas import tpu as pltpu

BF = jnp.bfloat16
F32 = jnp.float32
I32 = jnp.int32

HEAD_DIM = 128
H_A, H_B, H_C = 6, 5, 5
DA_QK = 64
H_IDX, D_IDX = 16, 64
TOPK_MAX = 256
ROPE_THETA = 500000.0
ROT_FRAC = 4
TOP_K_EXPERTS = 4
SWIGLU_LIMIT = 7.0
SWIGLU_ALPHA = 1.702
LN_EPS = 1e-5
RMS_EPS = 1e-6
LANE = 128
NEG = -1e30
KEY_NEG_INF = -2139095041
VMEM_LIMIT = 56 * 1024 * 1024
W_A = H_A * HEAD_DIM
W_B = H_B * HEAD_DIM
W_C = H_C * HEAD_DIM
W_QI = H_IDX * D_IDX
S_QA, S_KA, S_VA = 0, W_A, 2 * W_A
S_QB = 3 * W_A
S_KB, S_VB = S_QB + W_B, S_QB + 2 * W_B
S_QI = S_QB + 3 * W_B
S_WI = S_QI + W_QI
S_KI = S_WI + H_IDX
S_QC = S_KI + D_IDX
S_KC, S_VC = S_QC + W_C, S_QC + 2 * W_C
N_IN = S_QC + 3 * W_C
P_QA, P_KA, P_VA, P_QB, P_KB, P_VB, P_QI = S_QA, S_KA, S_VA, S_QB, S_KB, S_VB, S_QI
P_WI = P_QI + W_QI
P_KI = P_WI + LANE
P_QC = P_KI + LANE
P_KC, P_VC = P_QC + W_C, P_QC + 2 * W_C
N_PAD = P_QC + 3 * W_C


def _cparams(sem, vmem=VMEM_LIMIT):
    return pltpu.CompilerParams(dimension_semantics=sem, vmem_limit_bytes=vmem)


def _dot(a, b):
    return jnp.dot(a, b, preferred_element_type=F32)


def _dot_nt(a, b):
    return lax.dot_general(a, b, (((1,), (1,)), ((), ())), preferred_element_type=F32)


def _iota(shape, dim):
    return lax.broadcasted_iota(I32, shape, dim)


def _ada_kernel(c_ref, w_ref, b_ref, o_ref):
    c = c_ref[...]
    s = c * jax.nn.sigmoid(c)
    o_ref[...] = _dot(s, w_ref[...]) + b_ref[...]


def ada_modulation(c_all, w_ada, b_ada, tn=1024):
    depth, d, n = w_ada.shape
    rows = c_all.shape[0]
    return pl.pallas_call(
        _ada_kernel,
        grid=(depth, n // tn),
        in_specs=[
            pl.BlockSpec((rows, d), lambda l, j: (0, 0)),
            pl.BlockSpec((None, d, tn), lambda l, j: (l, 0, j)),
            pl.BlockSpec((None, 1, tn), lambda l, j: (l, 0, j)),
        ],
        out_specs=pl.BlockSpec((None, rows, tn), lambda l, j: (l, 0, j)),
        out_shape=jax.ShapeDtypeStruct((depth, rows, n), F32),
        compiler_params=_cparams(("parallel", "parallel")),
        name="ada_modulation",
    )(c_all, w_ada, b_ada.reshape(depth, 1, n))


def _proj_kernel(x_ref, sc_ref, sh_ref, w_ref, o_ref, h_sc):
    @pl.when(pl.program_id(2) == 0)
    def _():
        h_sc[...] = (x_ref[...] * (1.0 + sc_ref[...]) + sh_ref[...]).astype(BF)

    o_ref[...] = _dot(h_sc[...], w_ref[...])


def modulated_projection(x, sc, sh, w, layer, tm, tn=256):
    b, t, d = x.shape
    n = w.shape[-1]
    ts = sc.shape[1]
    tms = 1 if ts == 1 else tm
    mod_map = (lambda bi, i, j: (bi, 0, 0)) if ts == 1 else (lambda bi, i, j: (bi, i, 0))
    return pl.pallas_call(
        _proj_kernel,
        grid=(b, t // tm, n // tn),
        in_specs=[
            pl.BlockSpec((None, tm, d), lambda bi, i, j: (bi, i, 0)),
            pl.BlockSpec((None, tms, d), mod_map),
            pl.BlockSpec((None, tms, d), mod_map),
            pl.BlockSpec((None, d, tn), lambda bi, i, j: (layer, 0, j)),
        ],
        out_specs=pl.BlockSpec((None, tm, tn), lambda bi, i, j: (bi, i, j)),
        out_shape=jax.ShapeDtypeStruct((b, t, n), F32),
        scratch_shapes=[pltpu.VMEM((tm, d), BF)],
        compiler_params=_cparams(("parallel", "parallel", "arbitrary")),
        name="modulated_projection",
    )(x, sc, sh, w)


def _rope(x, tab_ref, half):
    c = tab_ref[:, 0:LANE]
    sa = tab_ref[:, LANE:2 * LANE]
    sb = tab_ref[:, 2 * LANE:3 * LANE]
    outs = []
    for blk in range(x.shape[1] // LANE):
        xb = x[:, blk * LANE:(blk + 1) * LANE]
        outs.append(xb * c + pltpu.roll(xb, half, 1) * sa + pltpu.roll(xb, LANE - half, 1) * sb)
    return outs[0] if len(outs) == 1 else jnp.concatenate(outs, axis=1)


def _epilogue_kernel(p_ref, t64_ref, t128_ref, ln_ref,
                     kva_ref, kvb_ref, idx_ref, kvc_ref,
                     qa_ref, ka_ref, va_ref, qb_ref, kb_ref, vb_ref,
                     qi_ref, wi_ref, ki2_ref, qc_ref, kc_ref, vc_ref):
    h64 = DA_QK // ROT_FRAC // 2
    h128 = HEAD_DIM // ROT_FRAC // 2
    qa = _rope(p_ref[:, P_QA:P_QA + W_A], t64_ref, h64)
    ka = _rope(p_ref[:, P_KA:P_KA + W_A], t64_ref, h64)
    va = p_ref[:, P_VA:P_VA + W_A]
    qa_ref[...] = qa.astype(BF)
    ka_ref[...] = ka.astype(BF)
    va_ref[...] = va.astype(BF)
    kva_ref[:, 0:W_A] = ka
    kva_ref[:, W_A:2 * W_A] = va

    qb = _rope(p_ref[:, P_QB:P_QB + W_B], t128_ref, h128)
    kb = _rope(p_ref[:, P_KB:P_KB + W_B], t128_ref, h128)
    vb = p_ref[:, P_VB:P_VB + W_B]
    qb_ref[...] = qb.astype(BF)
    kb_ref[...] = kb.astype(BF)
    vb_ref[...] = vb.astype(BF)
    kvb_ref[:, 0:W_B] = kb
    kvb_ref[:, W_B:2 * W_B] = vb

    qi_ref[...] = _rope(p_ref[:, P_QI:P_QI + W_QI], t64_ref, h64).astype(BF)
    wi_ref[...] = p_ref[:, P_WI:P_WI + LANE] * (H_IDX ** -0.5)

    ki = p_ref[:, P_KI:P_KI + LANE]
    real = _iota((1, LANE), 1) < D_IDX
    mu = jnp.sum(ki, axis=1, keepdims=True) * (1.0 / D_IDX)
    cen = jnp.where(real, ki - mu, 0.0)
    var = jnp.sum(cen * cen, axis=1, keepdims=True) * (1.0 / D_IDX)
    kin = cen * lax.rsqrt(var + LN_EPS) * ln_ref[0:1, :] + ln_ref[1:2, :]
    kir = jnp.where(real, _rope(kin, t64_ref, h64), 0.0)
    idx_ref[...] = kir[:, 0:D_IDX]
    kib = kir.astype(BF)
    ki2_ref[:, 0:LANE] = kib
    ki2_ref[:, LANE:2 * LANE] = pltpu.roll(kir, D_IDX, 1).astype(BF)

    kc = p_ref[:, P_KC:P_KC + W_C]
    vc = p_ref[:, P_VC:P_VC + W_C]
    qc_ref[...] = p_ref[:, P_QC:P_QC + W_C].astype(BF)
    kc_ref[...] = kc.astype(BF)
    vc_ref[...] = vc.astype(BF)
    kvc_ref[:, 0:W_C] = kc
    kvc_ref[:, W_C:2 * W_C] = vc


def projection_epilogue(proj, tab64, tab128, ln_gb, tm):
    m = proj.shape[0]
    row = lambda w: pl.BlockSpec((tm, w), lambda i: (i, 0))
    shapes = [
        ((m, 2 * W_A), F32), ((m, 2 * W_B), F32), ((m, D_IDX), F32), ((m, 2 * W_C), F32),
        ((m, W_A), BF), ((m, W_A), BF), ((m, W_A), BF),
        ((m, W_B), BF), ((m, W_B), BF), ((m, W_B), BF),
        ((m, W_QI), BF), ((m, LANE), F32), ((m, 2 * LANE), BF),
        ((m, W_C), BF), ((m, W_C), BF), ((m, W_C), BF),
    ]
    return pl.pallas_call(
        _epilogue_kernel,
        grid=(m // tm,),
        in_specs=[row(N_PAD), row(3 * LANE), row(3 * LANE), pl.BlockSpec((2, LANE), lambda i: (0, 0))],
        out_specs=[row(s[0][1]) for s in shapes],
        out_shape=[jax.ShapeDtypeStruct(*s) for s in shapes],
        compiler_params=_cparams(("parallel",)),
        name="projection_epilogue",
    )(proj, tab64, tab128, ln_gb)


def rope_tables(pos, head_dim):
    r = head_dim // ROT_FRAC
    half = r // 2
    inv = ROPE_THETA ** (-(jnp.arange(half, dtype=F32) * 2.0 / r))
    ang = pos.astype(F32)[:, None] * inv[None, :]
    cos, sin = jnp.cos(ang), jnp.sin(ang)
    n = pos.shape[0]
    pad = jnp.zeros((n, head_dim - r), F32)
    zero = jnp.zeros((n, half), F32)
    c = jnp.concatenate([cos, cos, pad + 1.0], axis=1)
    sa = jnp.concatenate([zero, sin, pad], axis=1)
    sb = jnp.concatenate([-sin, zero, pad], axis=1)
    rep = LANE // head_dim
    return jnp.concatenate([jnp.tile(c, (1, rep)), jnp.tile(sa, (1, rep)), jnp.tile(sb, (1, rep))], axis=1)


def _resident_spec(t, w):
    return pl.BlockSpec((None, t, w), lambda bi, i: (bi, 0, 0), pipeline_mode=pl.Buffered(1))


def _lambda(lam_ref, lam_init):
    lp = lam_ref[...]
    a = jnp.sum(lp[0:1, :] * lp[1:2, :], axis=1, keepdims=True)
    b = jnp.sum(lp[2:3, :] * lp[3:4, :], axis=1, keepdims=True)
    return jnp.exp(a) - jnp.exp(b) + lam_init


def _flash_update(s, v, m, l, acc):
    mn = jnp.maximum(m, jnp.max(s, axis=1, keepdims=True))
    p = jnp.exp(s - mn)
    al = jnp.exp(m - mn)
    l = al * l + jnp.sum(p, axis=1, keepdims=True)
    acc = al * acc + _dot(p.astype(BF), v)
    return mn, l, acc


def _softplus(z):
    return jnp.maximum(z, 0.0) + jnp.log(1.0 + jnp.exp(-jnp.abs(z)))


def _suffix_matrix(n):
    return jnp.where(_iota((n, n), 0) > _iota((n, n), 1), 1.0, 0.0).astype(BF)


def _prefix_matrix(n):
    return jnp.where(_iota((n, n), 0) <= _iota((n, n), 1), 1.0, 0.0).astype(BF)


def _suffix_sum(x, mat):
    hi = x.astype(BF).astype(F32)
    both = _dot(jnp.concatenate([hi, x - hi], axis=0).astype(BF), mat)
    return both[:x.shape[0]] + both[x.shape[0]:]


def _sort_key(score):
    bits = pltpu.bitcast(score + 0.0, I32)
    return jnp.where(bits < 0, bits ^ jnp.int32(0x7FFFFFFF), bits)


def _kth_largest_key(count_ge, shape, k):
    t = jnp.full(shape, jnp.iinfo(jnp.int32).min, I32)
    for bit in range(31, -1, -1):
        step = jnp.int32(-2 ** 31) if bit == 31 else jnp.int32(1 << bit)
        cand = t + step
        t = jnp.where(count_ge(cand) >= float(k), cand, t)
    return t


def _diff_finish(acc, l, lam, gain, lam_init, rows_per_map):
    full = acc / l
    other = pltpu.roll(full, rows_per_map, 0)
    o = full - lam * other
    o = o * lax.rsqrt(jnp.mean(o * o, axis=1, keepdims=True) + RMS_EPS)
    return o * gain * (1.0 - lam_init)


def _attn_a_kernel(lam_ref, gain_ref, q_ref, k_ref, v_ref, o_ref, *, tq, tk, lam_init):
    assert tq == tk
    i = pl.program_id(1)
    lam = _lambda(lam_ref, lam_init)
    first = _iota((1, HEAD_DIM), 1) < DA_QK
    scale = DA_QK ** -0.5
    heads = [slice(h * HEAD_DIM, (h + 1) * HEAD_DIM) for h in range(H_A)]

    def chunk(c, carry, causal):
        off = pl.multiple_of(c * tk, tk)
        new = []
        for h, sl in enumerate(heads):
            q = q_ref[:, sl]
            zero = jnp.zeros_like(q)
            k = k_ref[pl.ds(off, tk), sl]
            v = v_ref[pl.ds(off, tk), sl]
            for c_map, qm in enumerate((jnp.where(first, q, zero), jnp.where(first, zero, q))):
                s = _dot_nt(qm, k) * scale
                if causal:
                    s = jnp.where(_iota((1, tk), 1) <= _iota((tq, 1), 0), s, NEG)
                new.append(_flash_update(s, v, *carry[2 * h + c_map]))
        return tuple(new)

    init = (jnp.full((tq, 1), NEG, F32), jnp.zeros((tq, 1), F32), jnp.zeros((tq, HEAD_DIM), F32))
    res = lax.fori_loop(0, i, lambda c, carry: chunk(c, carry, False), (init,) * (2 * H_A))
    res = chunk(i, res, True)
    for h, sl in enumerate(heads):
        (_, l0, a0), (_, l1, a1) = res[2 * h], res[2 * h + 1]
        o = a0 / l0 - lam * (a1 / l1)
        o = o * lax.rsqrt(jnp.mean(o * o, axis=1, keepdims=True) + RMS_EPS)
        o_ref[:, sl] = (o * gain_ref[:, sl] * (1.0 - lam_init)).astype(BF)


def prompt_attention_a(qa, ka, va, lam_params, gain, layer, lam_init, tq, tk):
    b, t, _ = qa.shape
    kern = functools.partial(_attn_a_kernel, tq=tq, tk=tk, lam_init=lam_init)
    return pl.pallas_call(
        kern,
        grid=(b, t // tq),
        in_specs=[
            pl.BlockSpec((None, 4, DA_QK), lambda bi, i: (layer, 0, 0)),
            pl.BlockSpec((None, 1, W_A), lambda bi, i: (layer, 0, 0)),
            pl.BlockSpec((None, tq, W_A), lambda bi, i: (bi, i, 0)),
            _resident_spec(t, W_A),
            _resident_spec(t, W_A),
        ],
        out_specs=pl.BlockSpec((None, tq, W_A), lambda bi, i: (bi, i, 0)),
        out_shape=jax.ShapeDtypeStruct((b, t, W_A), BF),
        compiler_params=_cparams(("parallel", "arbitrary")),
        name="prompt_attention_a",
    )(lam_params, gain, qa, ka, va)


def _stick_chunk(q, k, v, before, mat, run, acc, scale):
    z = _dot_nt(q, k) * scale
    sp = _softplus(z)
    lm = jnp.where(before, -sp, 0.0)
    tail = _suffix_sum(lm, mat)
    log_a = jnp.where(before, z - sp + tail + run, NEG)
    acc = acc + _dot(jnp.exp(log_a).astype(BF), v)
    run = run + jnp.sum(lm, axis=1, keepdims=True)
    return run, acc


def _attn_c_kernel(q_ref, k_ref, v_ref, o_ref, *, tq, tk, unroll):
    i = pl.program_id(1)
    nch = ((i + 1) * tq) // tk
    qpos = i * tq + _iota((tq, 1), 0)
    mat = _suffix_matrix(tk)
    scale = HEAD_DIM ** -0.5

    def body(it, carry):
        for u in range(unroll):
            off = pl.multiple_of((nch - 1 - unroll * it - u) * tk, tk)
            before = (off + _iota((1, tk), 1)) < qpos
            new = []
            for h in range(H_C):
                sl = slice(h * HEAD_DIM, (h + 1) * HEAD_DIM)
                new.append(_stick_chunk(q_ref[:, sl], k_ref[pl.ds(off, tk), sl], v_ref[pl.ds(off, tk), sl],
                                        before, mat, *carry[h], scale))
            carry = tuple(new)
        return carry

    init = (jnp.zeros((tq, 1), F32), jnp.zeros((tq, HEAD_DIM), F32))
    res = lax.fori_loop(0, nch // unroll, body, (init,) * H_C)
    for h in range(H_C):
        o_ref[:, h * HEAD_DIM:(h + 1) * HEAD_DIM] = res[h][1].astype(BF)


def prompt_attention_c(qc, kc, vc, tq, tk, unroll=None):
    b, t, _ = qc.shape
    unroll = tq // tk if unroll is None else unroll
    kern = functools.partial(_attn_c_kernel, tq=tq, tk=tk, unroll=unroll)
    return pl.pallas_call(
        kern,
        grid=(b, t // tq),
        in_specs=[
            pl.BlockSpec((None, tq, W_C), lambda bi, i: (bi, i, 0)),
            _resident_spec(t, W_C),
            _resident_spec(t, W_C),
        ],
        out_specs=pl.BlockSpec((None, tq, W_C), lambda bi, i: (bi, i, 0)),
        out_shape=jax.ShapeDtypeStruct((b, t, W_C), BF),
        compiler_params=_cparams(("parallel", "arbitrary")),
        name="prompt_attention_c",
    )(qc, kc, vc)


def _indexer_scores(qi_ref, wib_sc, ki2):
    k_even = ki2[:, 0:LANE]
    k_odd = ki2[:, LANE:2 * LANE]
    nblk = ki2.shape[0] // LANE
    acc = [None] * nblk
    for hb in range(H_IDX // 2):
        qblk = qi_ref[:, hb * LANE:(hb + 1) * LANE]
        for par, kk in ((0, k_even), (1, k_odd)):
            w = wib_sc[2 * hb + par]
            d = jnp.maximum(_dot_nt(qblk, kk), 0.0)
            for b in range(nblk):
                t = d[:, b * LANE:(b + 1) * LANE] * w
                acc[b] = t if acc[b] is None else acc[b] + t
    return acc


def _select_mask(key, t, need, carry, pmat):
    gt = key > t
    eq = key == t
    eqf = jnp.where(eq, 1.0, 0.0)
    rank = _dot(eqf.astype(BF), pmat) + carry
    take = jnp.where(gt, 1.0, jnp.where(rank <= need, eqf, 0.0))
    take = jnp.where(key > KEY_NEG_INF, take, 0.0)
    return jnp.where(take > 0.5, 0.0, NEG), carry + jnp.sum(eqf, axis=1, keepdims=True)


def _attn_b_kernel(q_ref, qi_ref, wi_ref, k_ref, v_ref, ki2_ref, o_ref, wib_sc, key_sc, sel_sc, *, tq, topk):
    tk = tq
    nb = tk // LANE
    i = pl.program_id(1)
    nch = i + 1
    qpos = i * tq + _iota((tq, 1), 0)
    wi = wi_ref[...] * (D_IDX ** -0.5)
    for h in range(H_IDX):
        wib_sc[h] = jnp.broadcast_to(wi[:, h:h + 1], (tq, LANE))

    def score_body(c, _):
        off = pl.multiple_of(c * tk, tk)
        blocks = _indexer_scores(qi_ref, wib_sc, ki2_ref[pl.ds(off, tk), :])
        for b, sc in enumerate(blocks):
            vis = (off + b * LANE + _iota((1, LANE), 1)) <= qpos
            key_sc[c * nb + b] = _sort_key(jnp.where(vis, sc, -jnp.inf))
        return 0

    lax.fori_loop(0, nch, score_body, 0)

    def count(cmp, ref):
        parts = []
        for r in range(tq // LANE):
            rows = pl.ds(r * LANE, LANE)
            refb = jnp.broadcast_to(ref[r * LANE:(r + 1) * LANE], (LANE, LANE))

            def body(c, a, rows=rows, refb=refb):
                for b in range(nb):
                    a = a + jnp.where(cmp(key_sc[c * nb + b, rows, :], refb), 1.0, 0.0)
                return a

            parts.append(lax.fori_loop(0, nch, body, jnp.zeros((LANE, LANE), F32)))
        return jnp.sum(jnp.concatenate(parts, axis=0), axis=1, keepdims=True)

    t = _kth_largest_key(lambda cand: count(lambda k, c: k >= c, cand), (tq, 1), topk)
    need = float(topk) - count(lambda k, c: k > c, t)
    pmat = _prefix_matrix(LANE)

    def sel_body(c, carry):
        sel_sc[c], carry = _select_mask(key_sc[c], t, need, carry, pmat)
        return carry

    lax.fori_loop(0, nch * nb, sel_body, jnp.zeros((tq, 1), F32))

    scale = HEAD_DIM ** -0.5

    def body(c, carry):
        off = pl.multiple_of(c * tk, tk)
        mask = jnp.concatenate([sel_sc[c * nb + b] for b in range(nb)], axis=1)
        new = []
        for h in range(H_B):
            sl = slice(h * HEAD_DIM, (h + 1) * HEAD_DIM)
            s = _dot_nt(q_ref[:, sl], k_ref[pl.ds(off, tk), sl]) * scale + mask
            new.append(_flash_update(s, v_ref[pl.ds(off, tk), sl], *carry[h]))
        return tuple(new)

    init = (jnp.full((tq, 1), NEG, F32), jnp.zeros((tq, 1), F32), jnp.zeros((tq, HEAD_DIM), F32))
    res = lax.fori_loop(0, nch, body, (init,) * H_B)
    for h in range(H_B):
        _, l, acc = res[h]
        o_ref[:, h * HEAD_DIM:(h + 1) * HEAD_DIM] = (acc / l).astype(BF)


def prompt_attention_b(qb, qi, wi, kb, vb, ki2, topk, tq):
    b, t, _ = qb.shape
    kern = functools.partial(_attn_b_kernel, tq=tq, topk=topk)
    qspec = lambda w: pl.BlockSpec((None, tq, w), lambda bi, i: (bi, i, 0))
    kspec = lambda w: _resident_spec(t, w)
    return pl.pallas_call(
        kern,
        grid=(b, t // tq),
        in_specs=[qspec(W_B), qspec(W_QI), qspec(LANE), kspec(W_B), kspec(W_B), kspec(2 * LANE)],
        out_specs=qspec(W_B),
        out_shape=jax.ShapeDtypeStruct((b, t, W_B), BF),
        scratch_shapes=[pltpu.VMEM((H_IDX, tq, LANE), F32),
                        pltpu.VMEM((t // LANE, tq, LANE), I32), pltpu.VMEM((t // LANE, tq, LANE), F32)],
        compiler_params=_cparams(("parallel", "arbitrary")),
        name="prompt_attention_b",
    )(qb, qi, wi, kb, vb, ki2)


ROWS = 8


def _page_heads(page_refs, kv, h, heads):
    return [r[pl.ds(2 * h + kv, LANE, stride=2 * heads), :].astype(BF) for r in page_refs]


def flat_page_view(cache):
    depth, pool, slots, _, heads, hd = cache.shape
    return jnp.transpose(cache, (0, 1, 2, 4, 3, 5)).reshape(depth, pool, slots * heads * 2, hd)


def _paged_softmax_kernel(pt_ref, *refs, n_groups, group, heads, kind, lam_init, t_new):
    del pt_ref
    if kind == "a":
        lam_ref, gain_ref, q_ref, kn_ref, vn_ref = refs[:5]
        rest = refs[5:]
        mask_ref = None
    else:
        q_ref, kn_ref, vn_ref, mask_ref = refs[:4]
        rest = refs[4:]
    page_refs = rest[:group]
    o_ref, m_sc, l_sc, acc_sc = rest[group:]
    j = pl.program_id(1)
    scale = (DA_QK if kind == "a" else HEAD_DIM) ** -0.5

    @pl.when(j == 0)
    def _():
        m_sc[...] = jnp.full(m_sc.shape, NEG, F32)
        l_sc[...] = jnp.zeros(l_sc.shape, F32)
        acc_sc[...] = jnp.zeros(acc_sc.shape, F32)

    state = [(m_sc[h][:, 0:1], l_sc[h][:, 0:1], acc_sc[h]) for h in range(heads)]

    @pl.when(j < n_groups)
    def _():
        new = []
        for h in range(heads):
            q = q_ref[h]
            ks = _page_heads(page_refs, 0, h, heads)
            vs = _page_heads(page_refs, 1, h, heads)
            s = jnp.concatenate([_dot_nt(q, k) for k in ks], axis=1) * scale
            if mask_ref is not None:
                s = s + jnp.concatenate([mask_ref[g] for g in range(group)], axis=1)
            new.append(_flash_update(s, jnp.concatenate(vs, axis=0), *state[h]))
        for h, (m, l, acc) in enumerate(new):
            m_sc[h] = jnp.broadcast_to(m, (ROWS, LANE))
            l_sc[h] = jnp.broadcast_to(l, (ROWS, LANE))
            acc_sc[h] = acc

    @pl.when(j == n_groups)
    def _():
        if kind == "a":
            lam = _lambda(lam_ref, lam_init)
            row = _iota((ROWS, LANE), 0)
            ok = _iota((ROWS, LANE), 1) <= jnp.where(row >= t_new, row - t_new, row)
            new_mask = jnp.where(ok, 0.0, NEG)
        else:
            new_mask = mask_ref[0]
        for h in range(heads):
            sl = slice(h * HEAD_DIM, (h + 1) * HEAD_DIM)
            s = _dot_nt(q_ref[h], kn_ref[:, sl]) * scale + new_mask
            _, l, acc = _flash_update(s, vn_ref[:, sl], *state[h])
            if kind == "a":
                o_ref[h] = _diff_finish(acc, l, lam, gain_ref[:, sl], lam_init, t_new)
            else:
                o_ref[h] = acc / l


def sample_paged_softmax(kind, page_table, cache, layer, q, k_new, v_new, *, mask=None,
                         lam_params=None, gain=None, lam_init=0.0, t_new=4, group=4):
    bsz, heads = q.shape[0], q.shape[1]
    n_pages = page_table.shape[1]
    n_groups = n_pages // group
    hw = heads * HEAD_DIM
    kern = functools.partial(_paged_softmax_kernel, n_groups=n_groups, group=group, heads=heads,
                             kind=kind, lam_init=lam_init, t_new=t_new)

    def page_spec(g):
        return pl.BlockSpec(
            (None, None, LANE * 2 * heads, HEAD_DIM),
            lambda b, j, pt: (layer, pt[b, jnp.minimum(j * group + g, n_pages - 1)], 0, 0))

    in_specs = []
    args = []
    if kind == "a":
        in_specs += [pl.BlockSpec((None, 4, DA_QK), lambda b, j, pt: (layer, 0, 0)),
                     pl.BlockSpec((None, 1, hw), lambda b, j, pt: (layer, 0, 0))]
        args += [lam_params, gain]
    in_specs += [pl.BlockSpec((None, heads, ROWS, HEAD_DIM), lambda b, j, pt: (b, 0, 0, 0)),
                 pl.BlockSpec((None, LANE, hw), lambda b, j, pt: (b, 0, 0)),
                 pl.BlockSpec((None, LANE, hw), lambda b, j, pt: (b, 0, 0))]
    args += [q, k_new, v_new]
    if kind != "a":
        in_specs.append(pl.BlockSpec((None, group, ROWS, LANE), lambda b, j, pt: (b, j, 0, 0)))
        args.append(mask)
    in_specs += [page_spec(g) for g in range(group)]
    args += [cache] * group
    return pl.pallas_call(
        kern,
        grid_spec=pltpu.PrefetchScalarGridSpec(
            num_scalar_prefetch=1,
            grid=(bsz, n_groups + 1),
            in_specs=in_specs,
            out_specs=pl.BlockSpec((None, heads, ROWS, HEAD_DIM), lambda b, j, pt: (b, 0, 0, 0)),
            scratch_shapes=[pltpu.VMEM((heads, ROWS, LANE), F32)] * 3,
        ),
        out_shape=jax.ShapeDtypeStruct((bsz, heads, ROWS, HEAD_DIM), F32),
        compiler_params=_cparams(("parallel", "arbitrary")),
        name="sample_paged_softmax_" + kind,
    )(page_table, *args)


def _paged_stick_kernel(pt_ref, q_ref, kn_ref, vn_ref, *rest, group, heads, t_new):
    del pt_ref
    page_refs = rest[:group]
    o_ref, run_sc, acc_sc = rest[group:]
    j = pl.program_id(1)
    scale = HEAD_DIM ** -0.5
    mat = _suffix_matrix(LANE)

    def store(new):
        for h, (run, acc) in enumerate(new):
            run_sc[h] = jnp.broadcast_to(run, (ROWS, LANE))
            acc_sc[h] = acc

    @pl.when(j == 0)
    def _():
        before = _iota((ROWS, LANE), 1) < jnp.minimum(_iota((ROWS, LANE), 0), t_new)
        zero = (jnp.zeros((ROWS, 1), F32), jnp.zeros((ROWS, HEAD_DIM), F32))
        store([_stick_chunk(q_ref[h], kn_ref[:, h * HEAD_DIM:(h + 1) * HEAD_DIM],
                            vn_ref[:, h * HEAD_DIM:(h + 1) * HEAD_DIM], before, mat, *zero, scale)
               for h in range(heads)])

    @pl.when(j > 0)
    def _():
        state = [(run_sc[h][:, 0:1], acc_sc[h]) for h in range(heads)]
        new = []
        for h in range(heads):
            q = q_ref[h]
            ks = _page_heads(page_refs, 0, h, heads)
            vs = _page_heads(page_refs, 1, h, heads)
            z = jnp.concatenate([_dot_nt(q, k) for k in ks], axis=0) * scale
            sp = _softplus(z)
            log_a = z - sp + _suffix_sum(-sp, mat)
            tot = jnp.sum(-sp, axis=1, keepdims=True)
            run, acc = state[h]
            for g in range(group):
                rows = slice(g * ROWS, (g + 1) * ROWS)
                acc = acc + _dot(jnp.exp(log_a[rows] + run).astype(BF), vs[g])
                run = run + tot[rows]
            new.append((run, acc))
        store(new)

    @pl.when(j == pl.num_programs(1) - 1)
    def _():
        o_ref[...] = acc_sc[...]


def sample_paged_stick(page_table, cache, layer, q, k_new, v_new, *, t_new=4, group=4):
    bsz, heads = q.shape[0], q.shape[1]
    n_pages = page_table.shape[1]
    n_groups = n_pages // group
    hw = heads * HEAD_DIM
    kern = functools.partial(_paged_stick_kernel, group=group, heads=heads, t_new=t_new)

    def page_spec(g):
        return pl.BlockSpec(
            (None, None, LANE * 2 * heads, HEAD_DIM),
            lambda b, j, pt: (layer, pt[b, n_pages - 1 - (jnp.maximum(j - 1, 0) * group + g)], 0, 0))

    in_specs = [pl.BlockSpec((None, heads, ROWS, HEAD_DIM), lambda b, j, pt: (b, 0, 0, 0)),
                pl.BlockSpec((None, LANE, hw), lambda b, j, pt: (b, 0, 0)),
                pl.BlockSpec((None, LANE, hw), lambda b, j, pt: (b, 0, 0))]
    in_specs += [page_spec(g) for g in range(group)]
    return pl.pallas_call(
        kern,
        grid_spec=pltpu.PrefetchScalarGridSpec(
            num_scalar_prefetch=1,
            grid=(bsz, n_groups + 1),
            in_specs=in_specs,
            out_specs=pl.BlockSpec((None, heads, ROWS, HEAD_DIM), lambda b, j, pt: (b, 0, 0, 0)),
            scratch_shapes=[pltpu.VMEM((heads, ROWS, LANE), F32)] * 2,
        ),
        out_shape=jax.ShapeDtypeStruct((bsz, heads, ROWS, HEAD_DIM), F32),
        compiler_params=_cparams(("parallel", "arbitrary")),
        name="sample_paged_stick",
    )(page_table, q, k_new, v_new, *([cache] * group))


def _sample_score_kernel(pt_ref, qi_ref, wi_ref, kin_ref, *rest, n_groups, group, t_new):
    del pt_ref
    page_refs = rest[:group]
    o_ref = rest[group]
    j = pl.program_id(1)
    qi = qi_ref[...]
    wi = wi_ref[...]

    def score(dots):
        d = jnp.maximum(dots * (D_IDX ** -0.5), 0.0) * wi
        return jnp.sum(d.reshape(H_IDX, ROWS, LANE), axis=0)

    @pl.when(j < n_groups)
    def _():
        for g in range(group):
            o_ref[g] = score(_dot(qi, page_refs[g][...].astype(BF)))

    @pl.when(j == n_groups)
    def _():
        ok = (_iota((ROWS, LANE), 1) <= _iota((ROWS, LANE), 0)) & (_iota((ROWS, LANE), 1) < t_new)
        o_ref[0] = jnp.where(ok, score(_dot_nt(qi, kin_ref[...])), -jnp.inf)
        for g in range(1, group):
            o_ref[g] = jnp.full((ROWS, LANE), -jnp.inf, F32)


def sample_indexer_scores(page_table, cache_idx, layer, qi, wi, ki_new, *, t_new=4, group=4):
    bsz = qi.shape[0]
    n_pages = page_table.shape[1]
    n_groups = n_pages // group
    kern = functools.partial(_sample_score_kernel, n_groups=n_groups, group=group, t_new=t_new)

    def page_spec(g):
        return pl.BlockSpec(
            (None, None, D_IDX, LANE),
            lambda b, j, pt: (layer, pt[b, jnp.minimum(j * group + g, n_pages - 1)], 0, 0))

    in_specs = [pl.BlockSpec((None, H_IDX * ROWS, D_IDX), lambda b, j, pt: (b, 0, 0)),
                pl.BlockSpec((None, H_IDX * ROWS, 1), lambda b, j, pt: (b, 0, 0)),
                pl.BlockSpec((None, LANE, D_IDX), lambda b, j, pt: (b, 0, 0))]
    in_specs += [page_spec(g) for g in range(group)]
    return pl.pallas_call(
        kern,
        grid_spec=pltpu.PrefetchScalarGridSpec(
            num_scalar_prefetch=1,
            grid=(bsz, n_groups + 1),
            in_specs=in_specs,
            out_specs=pl.BlockSpec((None, group, ROWS, LANE), lambda b, j, pt: (b, j, 0, 0)),
        ),
        out_shape=jax.ShapeDtypeStruct((bsz, (n_groups + 1) * group, ROWS, LANE), F32),
        compiler_params=_cparams(("parallel", "arbitrary")),
        name="sample_indexer_scores",
    )(page_table, qi, wi, ki_new, *([cache_idx] * group))


def _sample_select_kernel(s_ref, o_ref, key_sc, *, topk):
    nch = s_ref.shape[0]

    def key_body(c, _):
        key_sc[c] = _sort_key(s_ref[c])
        return 0

    lax.fori_loop(0, nch, key_body, 0)

    unroll = 4 if nch % 4 == 0 else 1

    def count(pred):
        def body(c, a):
            for u in range(unroll):
                a = a + jnp.where(pred(key_sc[c * unroll + u]), 1.0, 0.0)
            return a
        cnt = lax.fori_loop(0, nch // unroll, body, jnp.zeros((ROWS, LANE), F32))
        return jnp.sum(cnt, axis=1, keepdims=True)

    def count_ge(cand):
        candb = jnp.broadcast_to(cand, (ROWS, LANE))
        return count(lambda k: k >= candb)

    t = _kth_largest_key(count_ge, (ROWS, 1), topk)
    tb = jnp.broadcast_to(t, (ROWS, LANE))
    need = float(topk) - count(lambda k: k > tb)
    pmat = _prefix_matrix(LANE)

    def sel_body(c, carry):
        o_ref[c], carry = _select_mask(key_sc[c], t, need, carry, pmat)
        return carry

    lax.fori_loop(0, nch, sel_body, jnp.zeros((ROWS, 1), F32))


def sample_select_mask(scores, topk):
    bsz, nch = scores.shape[:2]
    blk = pl.BlockSpec((None, nch, ROWS, LANE), lambda b: (b, 0, 0, 0))
    return pl.pallas_call(
        functools.partial(_sample_select_kernel, topk=topk),
        grid=(bsz,),
        in_specs=[blk],
        out_specs=blk,
        out_shape=jax.ShapeDtypeStruct(scores.shape, F32),
        scratch_shapes=[pltpu.VMEM((nch, ROWS, LANE), I32)],
        compiler_params=_cparams(("parallel",)),
        name="sample_select_mask",
    )(scores)


def _layer_norm(x, g, b):
    mu = jnp.mean(x, axis=1, keepdims=True)
    cen = x - mu
    var = jnp.mean(cen * cen, axis=1, keepdims=True)
    return cen * lax.rsqrt(var + LN_EPS) * g + b


def _out_kernel(oa_ref, ob_ref, oc_ref, w_ref, x_ref, ga_ref, lng_ref, lnb_ref, sc_ref, sh_ref,
                wr_ref, br_ref, x1_ref, h_ref, lg_ref, *, alpha):
    attn = (_dot(oa_ref[...], w_ref[0:W_A, :])
            + _dot(ob_ref[...], w_ref[W_A:W_A + W_B, :])
            + _dot(oc_ref[...], w_ref[W_A + W_B:W_A + W_B + W_C, :]))
    x1 = _layer_norm(alpha * x_ref[...] + (1.0 + ga_ref[...]) * attn, lng_ref[...], lnb_ref[...])
    x1_ref[...] = x1
    h = x1 * (1.0 + sc_ref[...]) + sh_ref[...]
    h_ref[...] = h
    hh = h.astype(BF)
    hl = (h - hh.astype(F32)).astype(BF)
    w = wr_ref[...]
    wh = w.astype(BF)
    wl = (w - wh.astype(F32)).astype(BF)
    lg_ref[...] = _dot(hh, wh) + _dot(hh, wl) + _dot(hl, wh) + br_ref[...]


def out_projection(oa, ob, oc, w_out, x, g_a, ln_g, ln_b, sc_f, sh_f, w_router, b_router, layer, alpha, tm):
    b, t, d = x.shape
    ts = g_a.shape[1]
    tms = 1 if ts == 1 else tm
    mod_map = (lambda bi, i: (bi, 0, 0)) if ts == 1 else (lambda bi, i: (bi, i, 0))
    ne = w_router.shape[-1]
    row = lambda w: pl.BlockSpec((None, tm, w), lambda bi, i: (bi, i, 0))
    mod = pl.BlockSpec((None, tms, d), mod_map)
    par = lambda w: pl.BlockSpec((None, 1, w), lambda bi, i: (layer, 0, 0))
    return pl.pallas_call(
        functools.partial(_out_kernel, alpha=alpha),
        grid=(b, t // tm),
        in_specs=[row(W_A), row(W_B), row(W_C),
                  pl.BlockSpec((None, w_out.shape[1], d), lambda bi, i: (layer, 0, 0)),
                  row(d), mod, par(d), par(d), mod, mod,
                  pl.BlockSpec((None, d, ne), lambda bi, i: (layer, 0, 0)), par(ne)],
        out_specs=[row(d), row(d), row(ne)],
        out_shape=[jax.ShapeDtypeStruct((b, t, d), F32), jax.ShapeDtypeStruct((b, t, d), F32),
                   jax.ShapeDtypeStruct((b, t, ne), F32)],
        compiler_params=_cparams(("parallel", "parallel")),
        name="out_projection",
    )(oa, ob, oc, w_out, x, g_a, ln_g, ln_b, sc_f, sh_f, w_router, b_router)


def _router_kernel(lg_ref, idx_ref, w_ref):
    lg = lg_ref[...]
    n, ne = lg.shape
    lane = _iota((n, ne), 1).astype(F32)
    vals, idxs = [], []
    for _ in range(TOP_K_EXPERTS):
        mx = jnp.max(lg, axis=1, keepdims=True)
        ix = jnp.min(jnp.where(lg == mx, lane, float(ne)), axis=1, keepdims=True)
        vals.append(mx)
        idxs.append(ix)
        lg = jnp.where(lane == ix, -jnp.inf, lg)
    es = [jnp.exp(v - vals[0]) for v in vals]
    tot = es[0]
    for e in es[1:]:
        tot = tot + e
    out_lane = _iota((n, TOP_K_EXPERTS), 1)
    idx = jnp.zeros((n, TOP_K_EXPERTS), F32)
    wgt = jnp.zeros((n, TOP_K_EXPERTS), F32)
    for k in range(TOP_K_EXPERTS):
        idx = jnp.where(out_lane == k, idxs[k], idx)
        wgt = jnp.where(out_lane == k, es[k] / tot, wgt)
    idx_ref[...] = idx.astype(I32)
    w_ref[...] = wgt


def router_topk(logits, tm):
    n, ne = logits.shape
    return pl.pallas_call(
        _router_kernel,
        grid=(n // tm,),
        in_specs=[pl.BlockSpec((tm, ne), lambda i: (i, 0))],
        out_specs=[pl.BlockSpec((tm, TOP_K_EXPERTS), lambda i: (i, 0))] * 2,
        out_shape=[jax.ShapeDtypeStruct((n, TOP_K_EXPERTS), I32), jax.ShapeDtypeStruct((n, TOP_K_EXPERTS), F32)],
        compiler_params=_cparams(("parallel",)),
        name="router_topk",
    )(logits)


def _gate_up_kernel(te_ref, tv_ref, x_ref, wg_ref, wu_ref, bg_ref, bu_ref, o_ref, wg_sc, wu_sc):
    i = pl.program_id(1)
    changed = jnp.logical_or(i == 0, te_ref[i] != te_ref[jnp.maximum(i - 1, 0)])

    @pl.when(changed)
    def _():
        wg_sc[...] = wg_ref[...].astype(BF)
        wu_sc[...] = wu_ref[...].astype(BF)

    @pl.when(tv_ref[i] > 0)
    def _():
        x = x_ref[...].astype(BF)
        g = jnp.minimum(_dot(x, wg_sc[...]) + bg_ref[...], SWIGLU_LIMIT)
        u = jnp.clip(_dot(x, wu_sc[...]) + bu_ref[...], -SWIGLU_LIMIT, SWIGLU_LIMIT)
        o_ref[...] = ((u + 1.0) * g * jax.nn.sigmoid(SWIGLU_ALPHA * g)).astype(BF)

    @pl.when(tv_ref[i] == 0)
    def _():
        o_ref[...] = jnp.zeros(o_ref.shape, BF)


def expert_gate_up(tile_expert, tile_valid, xs, w_gate_up, b_gate_up, layer, tm, tn):
    r, d = xs.shape
    dff = w_gate_up.shape[-1] // 2
    nj = dff // tn
    ne = w_gate_up.shape[1]
    bias = b_gate_up.reshape(b_gate_up.shape[0], ne, 1, 2 * dff)
    return pl.pallas_call(
        _gate_up_kernel,
        grid_spec=pltpu.PrefetchScalarGridSpec(
            num_scalar_prefetch=2,
            grid=(nj, r // tm),
            in_specs=[
                pl.BlockSpec((tm, d), lambda j, i, te, tv: (i, 0)),
                pl.BlockSpec((None, None, d, tn), lambda j, i, te, tv: (layer, te[i], 0, j)),
                pl.BlockSpec((None, None, d, tn), lambda j, i, te, tv: (layer, te[i], 0, j + nj)),
                pl.BlockSpec((None, None, 1, tn), lambda j, i, te, tv: (layer, te[i], 0, j)),
                pl.BlockSpec((None, None, 1, tn), lambda j, i, te, tv: (layer, te[i], 0, j + nj)),
            ],
            out_specs=pl.BlockSpec((tm, tn), lambda j, i, te, tv: (i, j)),
            scratch_shapes=[pltpu.VMEM((d, tn), BF)] * 2,
        ),
        out_shape=jax.ShapeDtypeStruct((r, dff), BF),
        compiler_params=_cparams(("arbitrary", "arbitrary")),
        name="expert_gate_up",
    )(tile_expert, tile_valid, xs, w_gate_up, w_gate_up, bias, bias)


def _down_kernel(te_ref, tv_ref, a_ref, w_ref, b_ref, o_ref, w_sc):
    i = pl.program_id(1)
    changed = jnp.logical_or(i == 0, te_ref[i] != te_ref[jnp.maximum(i - 1, 0)])

    @pl.when(changed)
    def _():
        w_sc[...] = w_ref[...].astype(BF)

    @pl.when(tv_ref[i] > 0)
    def _():
        o_ref[...] = _dot(a_ref[...], w_sc[...]) + b_ref[...]

    @pl.when(tv_ref[i] == 0)
    def _():
        o_ref[...] = jnp.zeros(o_ref.shape, F32)


def expert_down(tile_expert, tile_valid, act, w_down, b_down, layer, tm, tn):
    r, dff = act.shape
    d = w_down.shape[-1]
    ne = w_down.shape[1]
    bias = b_down.reshape(b_down.shape[0], ne, 1, d)
    return pl.pallas_call(
        _down_kernel,
        grid_spec=pltpu.PrefetchScalarGridSpec(
            num_scalar_prefetch=2,
            grid=(d // tn, r // tm),
            in_specs=[
                pl.BlockSpec((tm, dff), lambda j, i, te, tv: (i, 0)),
                pl.BlockSpec((None, None, dff, tn), lambda j, i, te, tv: (layer, te[i], 0, j)),
                pl.BlockSpec((None, None, 1, tn), lambda j, i, te, tv: (layer, te[i], 0, j)),
            ],
            out_specs=pl.BlockSpec((tm, tn), lambda j, i, te, tv: (i, j)),
            scratch_shapes=[pltpu.VMEM((dff, tn), BF)],
        ),
        out_shape=jax.ShapeDtypeStruct((r, d), F32),
        compiler_params=_cparams(("arbitrary", "arbitrary")),
        name="expert_down",
    )(tile_expert, tile_valid, act, w_down, bias)


def _ffn_norm_kernel(x_ref, y_ref, w_ref, g_ref, lng_ref, lnb_ref, o_ref, *, alpha):
    w = w_ref[...]
    ffn = y_ref[0] * w[:, 0:1]
    for k in range(1, TOP_K_EXPERTS):
        ffn = ffn + y_ref[k] * w[:, k:k + 1]
    o_ref[...] = _layer_norm(alpha * x_ref[...] + (1.0 + g_ref[...]) * ffn, lng_ref[...], lnb_ref[...])


def ffn_combine_norm(x1, y4, top_w, g_f, ln_g, ln_b, layer, alpha, tm):
    b, t, d = x1.shape
    ts = g_f.shape[1]
    tms = 1 if ts == 1 else tm
    mod_map = (lambda bi, i: (bi, 0, 0)) if ts == 1 else (lambda bi, i: (bi, i, 0))
    row = pl.BlockSpec((None, tm, d), lambda bi, i: (bi, i, 0))
    par = pl.BlockSpec((None, 1, d), lambda bi, i: (layer, 0, 0))
    return pl.pallas_call(
        functools.partial(_ffn_norm_kernel, alpha=alpha),
        grid=(b, t // tm),
        in_specs=[row, pl.BlockSpec((TOP_K_EXPERTS, None, tm, d), lambda bi, i: (0, bi, i, 0)),
                  pl.BlockSpec((None, tm, TOP_K_EXPERTS), lambda bi, i: (bi, i, 0)),
                  pl.BlockSpec((None, tms, d), mod_map), par, par],
        out_specs=row,
        out_shape=jax.ShapeDtypeStruct((b, t, d), F32),
        compiler_params=_cparams(("parallel", "parallel")),
        name="ffn_combine_norm",
    )(x1, y4, top_w, g_f, ln_g, ln_b)


def _route(top_idx, n_experts, tm):
    n, k = top_idx.shape
    flat = top_idx.reshape(-1)
    onehot = (flat[:, None] == jnp.arange(n_experts, dtype=I32)[None, :]).astype(I32)
    csum = jnp.cumsum(onehot, axis=0)
    sizes = csum[-1]
    padded = ((sizes + tm - 1) // tm) * tm
    pend = jnp.cumsum(padded)
    pstarts = pend - padded
    n_rows = ((n * k + n_experts * (tm - 1)) // tm + 1) * tm
    dest_flat = jnp.sum(onehot * (csum - 1 + pstarts[None, :]), axis=1)
    src_token = jnp.zeros((n_rows,), I32).at[dest_flat].set(jnp.arange(n * k, dtype=I32) // k)
    tile_start = jnp.arange(n_rows // tm, dtype=I32) * tm
    te = jnp.minimum(jnp.sum((pend[None, :] <= tile_start[:, None]).astype(I32), axis=1), n_experts - 1)
    tv = (tile_start < pend[-1]).astype(I32)
    return src_token, dest_flat.reshape(n, k), te, tv


def moe_experts(h_tok, top_idx, w_gate_up, b_gate_up, w_down, b_down, layer, tm_route):
    d = h_tok.shape[1]
    ne = w_gate_up.shape[1]
    dff = w_down.shape[2]
    src_token, dest, te, tv = _route(top_idx, ne, tm_route)
    xs = jnp.take(h_tok, src_token, axis=0)
    act = expert_gate_up(te, tv, xs, w_gate_up, b_gate_up, layer, tm_route, min(1024, dff))
    return expert_down(te, tv, act, w_down, b_down, layer, tm_route, min(1024, d)), dest


def _pad_w_in(w_in):
    depth, d, _ = w_in.shape
    z = lambda n: jnp.zeros((depth, d, n), w_in.dtype)
    return jnp.concatenate([
        w_in[:, :, :S_WI], w_in[:, :, S_WI:S_KI], z(LANE - H_IDX),
        w_in[:, :, S_KI:S_QC], z(LANE - D_IDX), w_in[:, :, S_QC:]], axis=2).astype(BF)


def _pick(n, prefs):
    for p in prefs:
        if n % p == 0:
            return p
    return n


def kernel(x_prompt, x_sample, cache_kv_a, cache_kv_b, cache_idx_b, cache_kv_c, page_table, c_prompt, c_sample,
           w_ada, b_ada, w_in, ln_idx_g, ln_idx_b, lam_params, gn_a_gain, w_out, ln1_g, ln1_b, w_router,
           b_router, w_gate_up, b_gate_up, w_down, b_down, ln2_g, ln2_b):
    depth = w_in.shape[0]
    bp, tp, d = x_prompt.shape
    bs, t_new, _ = x_sample.shape
    n_pages = page_table.shape[1]
    past_len = n_pages * cache_kv_a.shape[2]
    alpha = (2 * depth) ** 0.25
    topk_p = min(TOPK_MAX, tp // 4)
    topk_s = min(TOPK_MAX, (past_len + t_new) // 4)
    ne = w_router.shape[-1]
    group = _pick(n_pages, (8, 4, 2, 1))

    w_in_p = _pad_w_in(w_in)
    w_out_b = w_out.astype(BF)
    pad_lane = lambda v: jnp.pad(v, ((0, 0), (0, LANE - v.shape[1])))
    ln_gb = jnp.stack([pad_lane(ln_idx_g), pad_lane(ln_idx_b)], axis=1)
    gain = gn_a_gain.reshape(depth, 1, W_A)
    vec = lambda v: v.reshape(depth, 1, -1)

    pos_p = jnp.arange(tp, dtype=I32)
    pos_s = past_len + jnp.arange(t_new, dtype=I32)
    tab64_p = jnp.tile(rope_tables(pos_p, DA_QK), (bp, 1))
    tab128_p = jnp.tile(rope_tables(pos_p, HEAD_DIM), (bp, 1))
    tab64_s = jnp.tile(rope_tables(pos_s, DA_QK), (bs, 1))
    tab128_s = jnp.tile(rope_tables(pos_s, HEAD_DIM), (bs, 1))

    n_c = bp + bs
    rows_c = -(-n_c // 8) * 8
    c_all = jnp.pad(jnp.concatenate([c_prompt, c_sample], axis=0), ((0, rows_c - n_c), (0, 0)))
    mod = ada_modulation(c_all, w_ada, b_ada, tn=_pick(6 * d, (1024, 512, 256, 128))).reshape(depth, rows_c, 6, d)

    tm_proj = _pick(tp, (1024, 512, 256, 128))
    tm_epi = _pick(tp, (256, 128))
    tq_a = _pick(tp, (512, 256, 128))
    tq_b = _pick(tp, (256, 128))
    tm_out = _pick(tp, (256, 128))
    ms = bs * t_new
    kva_pages, kvb_pages, kvc_pages = [flat_page_view(c) for c in (cache_kv_a, cache_kv_b, cache_kv_c)]
    idx_pages = jnp.swapaxes(cache_idx_b, 2, 3)

    xp, xs = x_prompt, x_sample.reshape(1, ms, d)
    outs_p, outs_s = [], []
    first = jnp.arange(HEAD_DIM) < DA_QK
    for l in range(depth):
        lam_init = 0.8 - 0.6 * math.exp(-0.3 * l)
        mp = mod[l, :bp][:, :, None, :]
        msr = jnp.repeat(mod[l, bp:n_c], t_new, axis=0)[None]
        sh_a, sc_a, g_a, sh_f, sc_f, g_f = [mp[:, i] for i in range(6)]
        sh_as, sc_as, g_as, sh_fs, sc_fs, g_fs = [msr[:, :, i] for i in range(6)]

        proj = modulated_projection(xp, sc_a, sh_a, w_in_p, l, tm_proj).reshape(bp * tp, N_PAD)
        (kv_a, kv_b, idx, kv_c, qa, ka, va, qb, kb, vb, qi, wi, ki2, qc, kc, vc) = projection_epilogue(
            proj, tab64_p, tab128_p, ln_gb[l], tm_epi)
        r3 = lambda a: a.reshape(bp, tp, a.shape[-1])
        oa = prompt_attention_a(r3(qa), r3(ka), r3(va), lam_params, gain, l, lam_init, tq_a, tq_a)
        ob = prompt_attention_b(r3(qb), r3(qi), r3(wi), r3(kb), r3(vb), r3(ki2), topk_p, tq_b)
        oc = prompt_attention_c(r3(qc), r3(kc), r3(vc), tq_a, LANE)
        x1p, hp, lgp = out_projection(oa, ob, oc, w_out_b, xp, g_a, vec(ln1_g), vec(ln1_b), sc_f, sh_f,
                                      w_router, vec(b_router), l, alpha, tm_out)
        outs_p.append((kv_a.reshape(bp, tp, 2, H_A, HEAD_DIM), kv_b.reshape(bp, tp, 2, H_B, HEAD_DIM),
                       idx.reshape(bp, tp, D_IDX), kv_c.reshape(bp, tp, 2, H_C, HEAD_DIM)))

        proj_s = modulated_projection(xs, sc_as, sh_as, w_in_p, l, ms).reshape(ms, N_PAD)
        (kv_a_s, kv_b_s, idx_s, kv_c_s, qa_s, ka_s, va_s, qb_s, kb_s, vb_s, qi_s, wi_s, ki2_s, qc_s, kc_s,
         vc_s) = projection_epilogue(proj_s, tab64_s, tab128_s, ln_gb[l], ms)
        new_rows = lambda a: jnp.pad(a.reshape(bs, t_new, a.shape[-1]), ((0, 0), (0, LANE - t_new), (0, 0)))
        heads_first = lambda a, h: jnp.swapaxes(a.reshape(bs, t_new, h, HEAD_DIM), 1, 2)
        pad_rows = lambda a: jnp.pad(a, ((0, 0), (0, 0), (0, ROWS - t_new), (0, 0)))
        qa_h = heads_first(qa_s, H_A)
        qa_2 = jnp.concatenate([jnp.where(first, qa_h, 0), jnp.where(first, 0, qa_h)], axis=2).astype(BF)
        oa_s = sample_paged_softmax("a", page_table, kva_pages, l, qa_2, new_rows(ka_s), new_rows(va_s),
                                    lam_params=lam_params, gain=gain, lam_init=lam_init, t_new=t_new, group=group)
        qi_h = pad_rows(jnp.swapaxes(qi_s.reshape(bs, t_new, H_IDX, D_IDX), 1, 2)).reshape(bs, H_IDX * ROWS, D_IDX)
        wi_h = pad_rows(jnp.swapaxes(wi_s[:, :H_IDX].reshape(bs, t_new, H_IDX, 1), 1, 2)).reshape(
            bs, H_IDX * ROWS, 1)
        scores = sample_indexer_scores(page_table, idx_pages, l, qi_h, wi_h, new_rows(ki2_s[:, :D_IDX]),
                                       t_new=t_new, group=group)
        sel = sample_select_mask(scores, topk_s)
        ob_s = sample_paged_softmax("b", page_table, kvb_pages, l, pad_rows(heads_first(qb_s, H_B)),
                                    new_rows(kb_s), new_rows(vb_s), mask=sel, t_new=t_new, group=group)
        oc_s = sample_paged_stick(page_table, kvc_pages, l, pad_rows(heads_first(qc_s, H_C)),
                                  new_rows(kc_s), new_rows(vc_s), t_new=t_new, group=group)
        tok = lambda o, h: jnp.swapaxes(o[:, :, :t_new], 1, 2).reshape(1, ms, h * HEAD_DIM).astype(BF)
        x1s, hs, lgs = out_projection(tok(oa_s, H_A), tok(ob_s, H_B), tok(oc_s, H_C), w_out_b, xs, g_as,
                                      vec(ln1_g), vec(ln1_b), sc_fs, sh_fs, w_router, vec(b_router), l, alpha, ms)
        outs_s.append((kv_a_s.reshape(bs, t_new, 2, H_A, HEAD_DIM), kv_b_s.reshape(bs, t_new, 2, H_B, HEAD_DIM),
                       idx_s.reshape(bs, t_new, D_IDX), kv_c_s.reshape(bs, t_new, 2, H_C, HEAD_DIM)))

        np_ = bp * tp
        idx_p, w_p = router_topk(lgp.reshape(np_, ne), tm_out)
        idx_s, w_s = router_topk(lgs.reshape(ms, ne), ms)
        h_tok = jnp.concatenate([hp.reshape(np_, d), hs.reshape(ms, d)], axis=0)
        y, dest = moe_experts(h_tok, jnp.concatenate([idx_p, idx_s], axis=0), w_gate_up, b_gate_up,
                              w_down, b_down, l, 256)
        y4p = jnp.take(y, dest[:np_].T.reshape(TOP_K_EXPERTS, bp, tp), axis=0)
        y4s = jnp.take(y, dest[np_:].T.reshape(TOP_K_EXPERTS, 1, ms), axis=0)
        xp = ffn_combine_norm(x1p, y4p, w_p.reshape(bp, tp, TOP_K_EXPERTS), g_f, vec(ln2_g), vec(ln2_b),
                              l, alpha, tm_out)
        xs = ffn_combine_norm(x1s, y4s, w_s.reshape(1, ms, TOP_K_EXPERTS), g_fs, vec(ln2_g), vec(ln2_b),
                              l, alpha, ms)

    stack = lambda outs, i: jnp.stack([o[i] for o in outs])
    return (xp, xs.reshape(bs, t_new, d),
            stack(outs_p, 0), stack(outs_p, 1), stack(outs_p, 2), stack(outs_p, 3),
            stack(outs_s, 0), stack(outs_s, 1), stack(outs_s, 2), stack(outs_s, 3))
```

```python
import functools
import math

import jax
import jax.numpy as jnp
from jax import lax
from jax.experimental import pallas as pl
from jax.experimental.pallas import tpu as pltpu

BF = jnp.bfloat16
F32 = jnp.float32
I32 = jnp.int32

HEAD_DIM = 128
H_A, H_B, H_C = 6, 5, 5
DA_QK = 64
H_IDX, D_IDX = 16, 64
TOPK_MAX = 256
ROPE_THETA = 500000.0
ROT_FRAC = 4
TOP_K_EXPERTS = 4
SWIGLU_LIMIT = 7.0
SWIGLU_ALPHA = 1.702
LN_EPS = 1e-5
RMS_EPS = 1e-6
LANE = 128
NEG = -1e30
KEY_NEG_INF = -2139095041
VMEM_LIMIT = 56 * 1024 * 1024
W_A = H_A * HEAD_DIM
W_B = H_B * HEAD_DIM
W_C = H_C * HEAD_DIM
W_QI = H_IDX * D_IDX
S_QA, S_KA, S_VA = 0, W_A, 2 * W_A
S_QB = 3 * W_A
S_KB, S_VB = S_QB + W_B, S_QB + 2 * W_B
S_QI = S_QB + 3 * W_B
S_WI = S_QI + W_QI
S_KI = S_WI + H_IDX
S_QC = S_KI + D_IDX
S_KC, S_VC = S_QC + W_C, S_QC + 2 * W_C
N_IN = S_QC + 3 * W_C
P_QA, P_KA, P_VA, P_QB, P_KB, P_VB, P_QI = S_QA, S_KA, S_VA, S_QB, S_KB, S_VB, S_QI
P_WI = P_QI + W_QI
P_KI = P_WI + LANE
P_QC = P_KI + LANE
P_KC, P_VC = P_QC + W_C, P_QC + 2 * W_C
N_PAD = P_QC + 3 * W_C


def _cparams(sem, vmem=VMEM_LIMIT):
    return pltpu.CompilerParams(dimension_semantics=sem, vmem_limit_bytes=vmem)


def _dot(a, b):
    return jnp.dot(a, b, preferred_element_type=F32)


def _dot_nt(a, b):
    return lax.dot_general(a, b, (((1,), (1,)), ((), ())), preferred_element_type=F32)


def _iota(shape, dim):
    return lax.broadcasted_iota(I32, shape, dim)


def _ada_kernel(c_ref, w_ref, b_ref, o_ref):
    c = c_ref[...]
    s = c * jax.nn.sigmoid(c)
    o_ref[...] = _dot(s, w_ref[...]) + b_ref[...]


def ada_modulation(c_all, w_ada, b_ada, tn=1024):
    depth, d, n = w_ada.shape
    rows = c_all.shape[0]
    return pl.pallas_call(
        _ada_kernel,
        grid=(depth, n // tn),
        in_specs=[
            pl.BlockSpec((rows, d), lambda l, j: (0, 0)),
            pl.BlockSpec((None, d, tn), lambda l, j: (l, 0, j)),
            pl.BlockSpec((None, 1, tn), lambda l, j: (l, 0, j)),
        ],
        out_specs=pl.BlockSpec((None, rows, tn), lambda l, j: (l, 0, j)),
        out_shape=jax.ShapeDtypeStruct((depth, rows, n), F32),
        compiler_params=_cparams(("parallel", "parallel")),
        name="ada_modulation",
    )(c_all, w_ada, b_ada.reshape(depth, 1, n))


def _proj_kernel(x_ref, sc_ref, sh_ref, w_ref, o_ref, h_sc):
    @pl.when(pl.program_id(2) == 0)
    def _():
        h_sc[...] = (x_ref[...] * (1.0 + sc_ref[...]) + sh_ref[...]).astype(BF)

    o_ref[...] = _dot(h_sc[...], w_ref[...])


def modulated_projection(x, sc, sh, w, layer, tm, tn=256):
    b, t, d = x.shape
    n = w.shape[-1]
    ts = sc.shape[1]
    tms = 1 if ts == 1 else tm
    mod_map = (lambda bi, i, j: (bi, 0, 0)) if ts == 1 else (lambda bi, i, j: (bi, i, 0))
    return pl.pallas_call(
        _proj_kernel,
        grid=(b, t // tm, n // tn),
        in_specs=[
            pl.BlockSpec((None, tm, d), lambda bi, i, j: (bi, i, 0)),
            pl.BlockSpec((None, tms, d), mod_map),
            pl.BlockSpec((None, tms, d), mod_map),
            pl.BlockSpec((None, d, tn), lambda bi, i, j: (layer, 0, j)),
        ],
        out_specs=pl.BlockSpec((None, tm, tn), lambda bi, i, j: (bi, i, j)),
        out_shape=jax.ShapeDtypeStruct((b, t, n), F32),
        scratch_shapes=[pltpu.VMEM((tm, d), BF)],
        compiler_params=_cparams(("parallel", "parallel", "arbitrary")),
        name="modulated_projection",
    )(x, sc, sh, w)


def _rope(x, tab_ref, half):
    c = tab_ref[:, 0:LANE]
    sa = tab_ref[:, LANE:2 * LANE]
    sb = tab_ref[:, 2 * LANE:3 * LANE]
    outs = []
    for blk in range(x.shape[1] // LANE):
        xb = x[:, blk * LANE:(blk + 1) * LANE]
        outs.append(xb * c + pltpu.roll(xb, half, 1) * sa + pltpu.roll(xb, LANE - half, 1) * sb)
    return outs[0] if len(outs) == 1 else jnp.concatenate(outs, axis=1)


def _epilogue_kernel(p_ref, t64_ref, t128_ref, ln_ref,
                     kva_ref, kvb_ref, idx_ref, kvc_ref,
                     qa_ref, ka_ref, va_ref, qb_ref, kb_ref, vb_ref,
                     qi_ref, wi_ref, ki2_ref, qc_ref, kc_ref, vc_ref):
    h64 = DA_QK // ROT_FRAC // 2
    h128 = HEAD_DIM // ROT_FRAC // 2
    qa = _rope(p_ref[:, P_QA:P_QA + W_A], t64_ref, h64)
    ka = _rope(p_ref[:, P_KA:P_KA + W_A], t64_ref, h64)
    va = p_ref[:, P_VA:P_VA + W_A]
    qa_ref[...] = qa.astype(BF)
    ka_ref[...] = ka.astype(BF)
    va_ref[...] = va.astype(BF)
    kva_ref[:, 0:W_A] = ka
    kva_ref[:, W_A:2 * W_A] = va

    qb = _rope(p_ref[:, P_QB:P_QB + W_B], t128_ref, h128)
    kb = _rope(p_ref[:, P_KB:P_KB + W_B], t128_ref, h128)
    vb = p_ref[:, P_VB:P_VB + W_B]
    qb_ref[...] = qb.astype(BF)
    kb_ref[...] = kb.astype(BF)
    vb_ref[...] = vb.astype(BF)
    kvb_ref[:, 0:W_B] = kb
    kvb_ref[:, W_B:2 * W_B] = vb

    qi_ref[...] = _rope(p_ref[:, P_QI:P_QI + W_QI], t64_ref, h64).astype(BF)
    wi_ref[...] = p_ref[:, P_WI:P_WI + LANE] * (H_IDX ** -0.5)

    ki = p_ref[:, P_KI:P_KI + LANE]
    real = _iota((1, LANE), 1) < D_IDX
    mu = jnp.sum(ki, axis=1, keepdims=True) * (1.0 / D_IDX)
    cen = jnp.where(real, ki - mu, 0.0)
    var = jnp.sum(cen * cen, axis=1, keepdims=True) * (1.0 / D_IDX)
    kin = cen * lax.rsqrt(var + LN_EPS) * ln_ref[0:1, :] + ln_ref[1:2, :]
    kir = jnp.where(real, _rope(kin, t64_ref, h64), 0.0)
    idx_ref[...] = kir[:, 0:D_IDX]
    kib = kir.astype(BF)
    ki2_ref[:, 0:LANE] = kib
    ki2_ref[:, LANE:2 * LANE] = pltpu.roll(kir, D_IDX, 1).astype(BF)

    kc = p_ref[:, P_KC:P_KC + W_C]
    vc = p_ref[:, P_VC:P_VC + W_C]
    qc_ref[...] = p_ref[:, P_QC:P_QC + W_C].astype(BF)
    kc_ref[...] = kc.astype(BF)
    vc_ref[...] = vc.astype(BF)
    kvc_ref[:, 0:W_C] = kc
    kvc_ref[:, W_C:2 * W_C] = vc


def projection_epilogue(proj, tab64, tab128, ln_gb, tm):
    m = proj.shape[0]
    row = lambda w: pl.BlockSpec((tm, w), lambda i: (i, 0))
    shapes = [
        ((m, 2 * W_A), F32), ((m, 2 * W_B), F32), ((m, D_IDX), F32), ((m, 2 * W_C), F32),
        ((m, W_A), BF), ((m, W_A), BF), ((m, W_A), BF),
        ((m, W_B), BF), ((m, W_B), BF), ((m, W_B), BF),
        ((m, W_QI), BF), ((m, LANE), F32), ((m, 2 * LANE), BF),
        ((m, W_C), BF), ((m, W_C), BF), ((m, W_C), BF),
    ]
    return pl.pallas_call(
        _epilogue_kernel,
        grid=(m // tm,),
        in_specs=[row(N_PAD), row(3 * LANE), row(3 * LANE), pl.BlockSpec((2, LANE), lambda i: (0, 0))],
        out_specs=[row(s[0][1]) for s in shapes],
        out_shape=[jax.ShapeDtypeStruct(*s) for s in shapes],
        compiler_params=_cparams(("parallel",)),
        name="projection_epilogue",
    )(proj, tab64, tab128, ln_gb)


def rope_tables(pos, head_dim):
    r = head_dim // ROT_FRAC
    half = r // 2
    inv = ROPE_THETA ** (-(jnp.arange(half, dtype=F32) * 2.0 / r))
    ang = pos.astype(F32)[:, None] * inv[None, :]
    cos, sin = jnp.cos(ang), jnp.sin(ang)
    n = pos.shape[0]
    pad = jnp.zeros((n, head_dim - r), F32)
    zero = jnp.zeros((n, half), F32)
    c = jnp.concatenate([cos, cos, pad + 1.0], axis=1)
    sa = jnp.concatenate([zero, sin, pad], axis=1)
    sb = jnp.concatenate([-sin, zero, pad], axis=1)
    rep = LANE // head_dim
    return jnp.concatenate([jnp.tile(c, (1, rep)), jnp.tile(sa, (1, rep)), jnp.tile(sb, (1, rep))], axis=1)


def _resident_spec(t, w):
    return pl.BlockSpec((None, t, w), lambda bi, i: (bi, 0, 0), pipeline_mode=pl.Buffered(1))


def _lambda(lam_ref, lam_init):
    lp = lam_ref[...]
    a = jnp.sum(lp[0:1, :] * lp[1:2, :], axis=1, keepdims=True)
    b = jnp.sum(lp[2:3, :] * lp[3:4, :], axis=1, keepdims=True)
    return jnp.exp(a) - jnp.exp(b) + lam_init


def _flash_update(s, v, m, l, acc):
    mn = jnp.maximum(m, jnp.max(s, axis=1, keepdims=True))
    p = jnp.exp(s - mn)
    al = jnp.exp(m - mn)
    l = al * l + jnp.sum(p, axis=1, keepdims=True)
    acc = al * acc + _dot(p.astype(BF), v)
    return mn, l, acc


def _softplus(z):
    return jnp.maximum(z, 0.0) + jnp.log(1.0 + jnp.exp(-jnp.abs(z)))


def _suffix_matrix(n):
    return jnp.where(_iota((n, n), 0) > _iota((n, n), 1), 1.0, 0.0).astype(BF)


def _prefix_matrix(n):
    return jnp.where(_iota((n, n), 0) <= _iota((n, n), 1), 1.0, 0.0).astype(BF)


def _suffix_sum(x, mat):
    hi = x.astype(BF).astype(F32)
    both = _dot(jnp.concatenate([hi, x - hi], axis=0).astype(BF), mat)
    return both[:x.shape[0]] + both[x.shape[0]:]


def _sort_key(score):
    bits = pltpu.bitcast(score + 0.0, I32)
    return jnp.where(bits < 0, bits ^ jnp.int32(0x7FFFFFFF), bits)


def _kth_largest_key(count_ge, shape, k):
    t = jnp.full(shape, jnp.iinfo(jnp.int32).min, I32)
    for bit in range(31, -1, -1):
        step = jnp.int32(-2 ** 31) if bit == 31 else jnp.int32(1 << bit)
        cand = t + step
        t = jnp.where(count_ge(cand) >= float(k), cand, t)
    return t


def _diff_finish(acc, l, lam, gain, lam_init, rows_per_map):
    full = acc / l
    other = pltpu.roll(full, rows_per_map, 0)
    o = full - lam * other
    o = o * lax.rsqrt(jnp.mean(o * o, axis=1, keepdims=True) + RMS_EPS)
    return o * gain * (1.0 - lam_init)


def _attn_a_kernel(lam_ref, gain_ref, q_ref, k_ref, v_ref, o_ref, *, tq, tk, lam_init):
    assert tq == tk
    i = pl.program_id(1)
    lam = _lambda(lam_ref, lam_init)
    first = _iota((1, HEAD_DIM), 1) < DA_QK
    scale = DA_QK ** -0.5
    heads = [slice(h * HEAD_DIM, (h + 1) * HEAD_DIM) for h in range(H_A)]

    def chunk(c, carry, causal):
        off = pl.multiple_of(c * tk, tk)
        new = []
        for h, sl in enumerate(heads):
            q = q_ref[:, sl]
            zero = jnp.zeros_like(q)
            k = k_ref[pl.ds(off, tk), sl]
            v = v_ref[pl.ds(off, tk), sl]
            for c_map, qm in enumerate((jnp.where(first, q, zero), jnp.where(first, zero, q))):
                s = _dot_nt(qm, k) * scale
                if causal:
                    s = jnp.where(_iota((1, tk), 1) <= _iota((tq, 1), 0), s, NEG)
                new.append(_flash_update(s, v, *carry[2 * h + c_map]))
        return tuple(new)

    init = (jnp.full((tq, 1), NEG, F32), jnp.zeros((tq, 1), F32), jnp.zeros((tq, HEAD_DIM), F32))
    res = lax.fori_loop(0, i, lambda c, carry: chunk(c, carry, False), (init,) * (2 * H_A))
    res = chunk(i, res, True)
    for h, sl in enumerate(heads):
        (_, l0, a0), (_, l1, a1) = res[2 * h], res[2 * h + 1]
        o = a0 / l0 - lam * (a1 / l1)
        o = o * lax.rsqrt(jnp.mean(o * o, axis=1, keepdims=True) + RMS_EPS)
        o_ref[:, sl] = (o * gain_ref[:, sl] * (1.0 - lam_init)).astype(BF)


def prompt_attention_a(qa, ka, va, lam_params, gain, layer, lam_init, tq, tk):
    b, t, _ = qa.shape
    kern = functools.partial(_attn_a_kernel, tq=tq, tk=tk, lam_init=lam_init)
    return pl.pallas_call(
        kern,
        grid=(b, t // tq),
        in_specs=[
            pl.BlockSpec((None, 4, DA_QK), lambda bi, i: (layer, 0, 0)),
            pl.BlockSpec((None, 1, W_A), lambda bi, i: (layer, 0, 0)),
            pl.BlockSpec((None, tq, W_A), lambda bi, i: (bi, i, 0)),
            _resident_spec(t, W_A),
            _resident_spec(t, W_A),
        ],
        out_specs=pl.BlockSpec((None, tq, W_A), lambda bi, i: (bi, i, 0)),
        out_shape=jax.ShapeDtypeStruct((b, t, W_A), BF),
        compiler_params=_cparams(("parallel", "arbitrary")),
        name="prompt_attention_a",
    )(lam_params, gain, qa, ka, va)


def _stick_chunk(q, k, v, before, mat, run, acc, scale):
    z = _dot_nt(q, k) * scale
    sp = _softplus(z)
    lm = jnp.where(before, -sp, 0.0)
    tail = _suffix_sum(lm, mat)
    log_a = jnp.where(before, z - sp + tail + run, NEG)
    acc = acc + _dot(jnp.exp(log_a).astype(BF), v)
    run = run + jnp.sum(lm, axis=1, keepdims=True)
    return run, acc


def _attn_c_kernel(q_ref, k_ref, v_ref, o_ref, *, tq, tk, unroll):
    i = pl.program_id(1)
    nch = ((i + 1) * tq) // tk
    qpos = i * tq + _iota((tq, 1), 0)
    mat = _suffix_matrix(tk)
    scale = HEAD_DIM ** -0.5

    def body(it, carry):
        for u in range(unroll):
            off = pl.multiple_of((nch - 1 - unroll * it - u) * tk, tk)
            before = (off + _iota((1, tk), 1)) < qpos
            new = []
            for h in range(H_C):
                sl = slice(h * HEAD_DIM, (h + 1) * HEAD_DIM)
                new.append(_stick_chunk(q_ref[:, sl], k_ref[pl.ds(off, tk), sl], v_ref[pl.ds(off, tk), sl],
                                        before, mat, *carry[h], scale))
            carry = tuple(new)
        return carry

    init = (jnp.zeros((tq, 1), F32), jnp.zeros((tq, HEAD_DIM), F32))
    res = lax.fori_loop(0, nch // unroll, body, (init,) * H_C)
    for h in range(H_C):
        o_ref[:, h * HEAD_DIM:(h + 1) * HEAD_DIM] = res[h][1].astype(BF)


def prompt_attention_c(qc, kc, vc, tq, tk, unroll=None):
    b, t, _ = qc.shape
    unroll = tq // tk if unroll is None else unroll
    kern = functools.partial(_attn_c_kernel, tq=tq, tk=tk, unroll=unroll)
    return pl.pallas_call(
        kern,
        grid=(b, t // tq),
        in_specs=[
            pl.BlockSpec((None, tq, W_C), lambda bi, i: (bi, i, 0)),
            _resident_spec(t, W_C),
            _resident_spec(t, W_C),
        ],
        out_specs=pl.BlockSpec((None, tq, W_C), lambda bi, i: (bi, i, 0)),
        out_shape=jax.ShapeDtypeStruct((b, t, W_C), BF),
        compiler_params=_cparams(("parallel", "arbitrary")),
        name="prompt_attention_c",
    )(qc, kc, vc)


def _indexer_scores(qi_ref, wib_sc, ki2):
    k_even = ki2[:, 0:LANE]
    k_odd = ki2[:, LANE:2 * LANE]
    nblk = ki2.shape[0] // LANE
    acc = [None] * nblk
    for hb in range(H_IDX // 2):
        qblk = qi_ref[:, hb * LANE:(hb + 1) * LANE]
        for par, kk in ((0, k_even), (1, k_odd)):
            w = wib_sc[2 * hb + par]
            d = jnp.maximum(_dot_nt(qblk, kk), 0.0)
            for b in range(nblk):
                t = d[:, b * LANE:(b + 1) * LANE] * w
                acc[b] = t if acc[b] is None else acc[b] + t
    return acc


def _select_mask(key, t, need, carry, pmat):
    gt = key > t
    eq = key == t
    eqf = jnp.where(eq, 1.0, 0.0)
    rank = _dot(eqf.astype(BF), pmat) + carry
    take = jnp.where(gt, 1.0, jnp.where(rank <= need, eqf, 0.0))
    take = jnp.where(key > KEY_NEG_INF, take, 0.0)
    return jnp.where(take > 0.5, 0.0, NEG), carry + jnp.sum(eqf, axis=1, keepdims=True)


def _attn_b_kernel(q_ref, qi_ref, wi_ref, k_ref, v_ref, ki2_ref, o_ref, wib_sc, key_sc, sel_sc, *, tq, topk):
    tk = tq
    nb = tk // LANE
    i = pl.program_id(1)
    nch = i + 1
    qpos = i * tq + _iota((tq, 1), 0)
    wi = wi_ref[...] * (D_IDX ** -0.5)
    for h in range(H_IDX):
        wib_sc[h] = jnp.broadcast_to(wi[:, h:h + 1], (tq, LANE))

    def score_body(c, _):
        off = pl.multiple_of(c * tk, tk)
        blocks = _indexer_scores(qi_ref, wib_sc, ki2_ref[pl.ds(off, tk), :])
        for b, sc in enumerate(blocks):
            vis = (off + b * LANE + _iota((1, LANE), 1)) <= qpos
            key_sc[c * nb + b] = _sort_key(jnp.where(vis, sc, -jnp.inf))
        return 0

    lax.fori_loop(0, nch, score_body, 0)

    def count(cmp, ref):
        parts = []
        for r in range(tq // LANE):
            rows = pl.ds(r * LANE, LANE)
            refb = jnp.broadcast_to(ref[r * LANE:(r + 1) * LANE], (LANE, LANE))

            def body(c, a, rows=rows, refb=refb):
                for b in range(nb):
                    a = a + jnp.where(cmp(key_sc[c * nb + b, rows, :], refb), 1.0, 0.0)
                return a

            parts.append(lax.fori_loop(0, nch, body, jnp.zeros((LANE, LANE), F32)))
        return jnp.sum(jnp.concatenate(parts, axis=0), axis=1, keepdims=True)

    t = _kth_largest_key(lambda cand: count(lambda k, c: k >= c, cand), (tq, 1), topk)
    need = float(topk) - count(lambda k, c: k > c, t)
    pmat = _prefix_matrix(LANE)

    def sel_body(c, carry):
        sel_sc[c], carry = _select_mask(key_sc[c], t, need, carry, pmat)
        return carry

    lax.fori_loop(0, nch * nb, sel_body, jnp.zeros((tq, 1), F32))

    scale = HEAD_DIM ** -0.5

    def body(c, carry):
        off = pl.multiple_of(c * tk, tk)
        mask = jnp.concatenate([sel_sc[c * nb + b] for b in range(nb)], axis=1)
        new = []
        for h in range(H_B):
            sl = slice(h * HEAD_DIM, (h + 1) * HEAD_DIM)
            s = _dot_nt(q_ref[:, sl], k_ref[pl.ds(off, tk), sl]) * scale + mask
            new.append(_flash_update(s, v_ref[pl.ds(off, tk), sl], *carry[h]))
        return tuple(new)

    init = (jnp.full((tq, 1), NEG, F32), jnp.zeros((tq, 1), F32), jnp.zeros((tq, HEAD_DIM), F32))
    res = lax.fori_loop(0, nch, body, (init,) * H_B)
    for h in range(H_B):
        _, l, acc = res[h]
        o_ref[:, h * HEAD_DIM:(h + 1) * HEAD_DIM] = (acc / l).astype(BF)


def prompt_attention_b(qb, qi, wi, kb, vb, ki2, topk, tq):
    b, t, _ = qb.shape
    kern = functools.partial(_attn_b_kernel, tq=tq, topk=topk)
    qspec = lambda w: pl.BlockSpec((None, tq, w), lambda bi, i: (bi, i, 0))
    kspec = lambda w: _resident_spec(t, w)
    return pl.pallas_call(
        kern,
        grid=(b, t // tq),
        in_specs=[qspec(W_B), qspec(W_QI), qspec(LANE), kspec(W_B), kspec(W_B), kspec(2 * LANE)],
        out_specs=qspec(W_B),
        out_shape=jax.ShapeDtypeStruct((b, t, W_B), BF),
        scratch_shapes=[pltpu.VMEM((H_IDX, tq, LANE), F32),
                        pltpu.VMEM((t // LANE, tq, LANE), I32), pltpu.VMEM((t // LANE, tq, LANE), F32)],
        compiler_params=_cparams(("parallel", "arbitrary")),
        name="prompt_attention_b",
    )(qb, qi, wi, kb, vb, ki2)


ROWS = 8


def _page_heads(page_refs, kv, h, heads):
    return [r[pl.ds(2 * h + kv, LANE, stride=2 * heads), :].astype(BF) for r in page_refs]


def flat_page_view(cache):
    depth, pool, slots, _, heads, hd = cache.shape
    return jnp.transpose(cache, (0, 1, 2, 4, 3, 5)).reshape(depth, pool, slots * heads * 2, hd)


def _paged_softmax_kernel(pt_ref, *refs, n_groups, group, heads, kind, lam_init, t_new):
    del pt_ref
    if kind == "a":
        lam_ref, gain_ref, q_ref, kn_ref, vn_ref = refs[:5]
        rest = refs[5:]
        mask_ref = None
    else:
        q_ref, kn_ref, vn_ref, mask_ref = refs[:4]
        rest = refs[4:]
    page_refs = rest[:group]
    o_ref, m_sc, l_sc, acc_sc = rest[group:]
    j = pl.program_id(1)
    scale = (DA_QK if kind == "a" else HEAD_DIM) ** -0.5

    @pl.when(j == 0)
    def _():
        m_sc[...] = jnp.full(m_sc.shape, NEG, F32)
        l_sc[...] = jnp.zeros(l_sc.shape, F32)
        acc_sc[...] = jnp.zeros(acc_sc.shape, F32)

    state = [(m_sc[h][:, 0:1], l_sc[h][:, 0:1], acc_sc[h]) for h in range(heads)]

    @pl.when(j < n_groups)
    def _():
        new = []
        for h in range(heads):
            q = q_ref[h]
            ks = _page_heads(page_refs, 0, h, heads)
            vs = _page_heads(page_refs, 1, h, heads)
            s = jnp.concatenate([_dot_nt(q, k) for k in ks], axis=1) * scale
            if mask_ref is not None:
                s = s + jnp.concatenate([mask_ref[g] for g in range(group)], axis=1)
            new.append(_flash_update(s, jnp.concatenate(vs, axis=0), *state[h]))
        for h, (m, l, acc) in enumerate(new):
            m_sc[h] = jnp.broadcast_to(m, (ROWS, LANE))
            l_sc[h] = jnp.broadcast_to(l, (ROWS, LANE))
            acc_sc[h] = acc

    @pl.when(j == n_groups)
    def _():
        if kind == "a":
            lam = _lambda(lam_ref, lam_init)
            row = _iota((ROWS, LANE), 0)
            ok = _iota((ROWS, LANE), 1) <= jnp.where(row >= t_new, row - t_new, row)
            new_mask = jnp.where(ok, 0.0, NEG)
        else:
            new_mask = mask_ref[0]
        for h in range(heads):
            sl = slice(h * HEAD_DIM, (h + 1) * HEAD_DIM)
            s = _dot_nt(q_ref[h], kn_ref[:, sl]) * scale + new_mask
            _, l, acc = _flash_update(s, vn_ref[:, sl], *state[h])
            if kind == "a":
                o_ref[h] = _diff_finish(acc, l, lam, gain_ref[:, sl], lam_init, t_new)
            else:
                o_ref[h] = acc / l


def sample_paged_softmax(kind, page_table, cache, layer, q, k_new, v_new, *, mask=None,
                         lam_params=None, gain=None, lam_init=0.0, t_new=4, group=4):
    bsz, heads = q.shape[0], q.shape[1]
    n_pages = page_table.shape[1]
    n_groups = n_pages // group
    hw = heads * HEAD_DIM
    kern = functools.partial(_paged_softmax_kernel, n_groups=n_groups, group=group, heads=heads,
                             kind=kind, lam_init=lam_init, t_new=t_new)

    def page_spec(g):
        return pl.BlockSpec(
            (None, None, LANE * 2 * heads, HEAD_DIM),
            lambda b, j, pt: (layer, pt[b, jnp.minimum(j * group + g, n_pages - 1)], 0, 0))

    in_specs = []
    args = []
    if kind == "a":
        in_specs += [pl.BlockSpec((None, 4, DA_QK), lambda b, j, pt: (layer, 0, 0)),
                     pl.BlockSpec((None, 1, hw), lambda b, j, pt: (layer, 0, 0))]
        args += [lam_params, gain]
    in_specs += [pl.BlockSpec((None, heads, ROWS, HEAD_DIM), lambda b, j, pt: (b, 0, 0, 0)),
                 pl.BlockSpec((None, LANE, hw), lambda b, j, pt: (b, 0, 0)),
                 pl.BlockSpec((None, LANE, hw), lambda b, j, pt: (b, 0, 0))]
    args += [q, k_new, v_new]
    if kind != "a":
        in_specs.append(pl.BlockSpec((None, group, ROWS, LANE), lambda b, j, pt: (b, j, 0, 0)))
        args.append(mask)
    in_specs += [page_spec(g) for g in range(group)]
    args += [cache] * group
    return pl.pallas_call(
        kern,
        grid_spec=pltpu.PrefetchScalarGridSpec(
            num_scalar_prefetch=1,
            grid=(bsz, n_groups + 1),
            in_specs=in_specs,
            out_specs=pl.BlockSpec((None, heads, ROWS, HEAD_DIM), lambda b, j, pt: (b, 0, 0, 0)),
            scratch_shapes=[pltpu.VMEM((heads, ROWS, LANE), F32)] * 3,
        ),
        out_shape=jax.ShapeDtypeStruct((bsz, heads, ROWS, HEAD_DIM), F32),
        compiler_params=_cparams(("parallel", "arbitrary")),
        name="sample_paged_softmax_" + kind,
    )(page_table, *args)


def _paged_stick_kernel(pt_ref, q_ref, kn_ref, vn_ref, *rest, group, heads, t_new):
    del pt_ref
    page_refs = rest[:group]
    o_ref, run_sc, acc_sc = rest[group:]
    j = pl.program_id(1)
    scale = HEAD_DIM ** -0.5
    mat = _suffix_matrix(LANE)

    def store(new):
        for h, (run, acc) in enumerate(new):
            run_sc[h] = jnp.broadcast_to(run, (ROWS, LANE))
            acc_sc[h] = acc

    @pl.when(j == 0)
    def _():
        before = _iota((ROWS, LANE), 1) < jnp.minimum(_iota((ROWS, LANE), 0), t_new)
        zero = (jnp.zeros((ROWS, 1), F32), jnp.zeros((ROWS, HEAD_DIM), F32))
        store([_stick_chunk(q_ref[h], kn_ref[:, h * HEAD_DIM:(h + 1) * HEAD_DIM],
                            vn_ref[:, h * HEAD_DIM:(h + 1) * HEAD_DIM], before, mat, *zero, scale)
               for h in range(heads)])

    @pl.when(j > 0)
    def _():
        state = [(run_sc[h][:, 0:1], acc_sc[h]) for h in range(heads)]
        new = []
        for h in range(heads):
            q = q_ref[h]
            ks = _page_heads(page_refs, 0, h, heads)
            vs = _page_heads(page_refs, 1, h, heads)
            z = jnp.concatenate([_dot_nt(q, k) for k in ks], axis=0) * scale
            sp = _softplus(z)
            log_a = z - sp + _suffix_sum(-sp, mat)
            tot = jnp.sum(-sp, axis=1, keepdims=True)
            run, acc = state[h]
            for g in range(group):
                rows = slice(g * ROWS, (g + 1) * ROWS)
                acc = acc + _dot(jnp.exp(log_a[rows] + run).astype(BF), vs[g])
                run = run + tot[rows]
            new.append((run, acc))
        store(new)

    @pl.when(j == pl.num_programs(1) - 1)
    def _():
        o_ref[...] = acc_sc[...]


def sample_paged_stick(page_table, cache, layer, q, k_new, v_new, *, t_new=4, group=4):
    bsz, heads = q.shape[0], q.shape[1]
    n_pages = page_table.shape[1]
    n_groups = n_pages // group
    hw = heads * HEAD_DIM
    kern = functools.partial(_paged_stick_kernel, group=group, heads=heads, t_new=t_new)

    def page_spec(g):
        return pl.BlockSpec(
            (None, None, LANE * 2 * heads, HEAD_DIM),
            lambda b, j, pt: (layer, pt[b, n_pages - 1 - (jnp.maximum(j - 1, 0) * group + g)], 0, 0))

    in_specs = [pl.BlockSpec((None, heads, ROWS, HEAD_DIM), lambda b, j, pt: (b, 0, 0, 0)),
                pl.BlockSpec((None, LANE, hw), lambda b, j, pt: (b, 0, 0)),
                pl.BlockSpec((None, LANE, hw), lambda b, j, pt: (b, 0, 0))]
    in_specs += [page_spec(g) for g in range(group)]
    return pl.pallas_call(
        kern,
        grid_spec=pltpu.PrefetchScalarGridSpec(
            num_scalar_prefetch=1,
            grid=(bsz, n_groups + 1),
            in_specs=in_specs,
            out_specs=pl.BlockSpec((None, heads, ROWS, HEAD_DIM), lambda b, j, pt: (b, 0, 0, 0)),
            scratch_shapes=[pltpu.VMEM((heads, ROWS, LANE), F32)] * 2,
        ),
        out_shape=jax.ShapeDtypeStruct((bsz, heads, ROWS, HEAD_DIM), F32),
        compiler_params=_cparams(("parallel", "arbitrary")),
        name="sample_paged_stick",
    )(page_table, q, k_new, v_new, *([cache] * group))


def _sample_score_kernel(pt_ref, qi_ref, wi_ref, kin_ref, *rest, n_groups, group, t_new):
    del pt_ref
    page_refs = rest[:group]
    o_ref = rest[group]
    j = pl.program_id(1)
    qi = qi_ref[...]
    wi = wi_ref[...]

    def score(dots):
        d = jnp.maximum(dots * (D_IDX ** -0.5), 0.0) * wi
        return jnp.sum(d.reshape(H_IDX, ROWS, LANE), axis=0)

    @pl.when(j < n_groups)
    def _():
        for g in range(group):
            o_ref[g] = score(_dot(qi, page_refs[g][...].astype(BF)))

    @pl.when(j == n_groups)
    def _():
        ok = (_iota((ROWS, LANE), 1) <= _iota((ROWS, LANE), 0)) & (_iota((ROWS, LANE), 1) < t_new)
        o_ref[0] = jnp.where(ok, score(_dot_nt(qi, kin_ref[...])), -jnp.inf)
        for g in range(1, group):
            o_ref[g] = jnp.full((ROWS, LANE), -jnp.inf, F32)


def sample_indexer_scores(page_table, cache_idx, layer, qi, wi, ki_new, *, t_new=4, group=4):
    bsz = qi.shape[0]
    n_pages = page_table.shape[1]
    n_groups = n_pages // group
    kern = functools.partial(_sample_score_kernel, n_groups=n_groups, group=group, t_new=t_new)

    def page_spec(g):
        return pl.BlockSpec(
            (None, None, D_IDX, LANE),
            lambda b, j, pt: (layer, pt[b, jnp.minimum(j * group + g, n_pages - 1)], 0, 0))

    in_specs = [pl.BlockSpec((None, H_IDX * ROWS, D_IDX), lambda b, j, pt: (b, 0, 0)),
                pl.BlockSpec((None, H_IDX * ROWS, 1), lambda b, j, pt: (b, 0, 0)),
                pl.BlockSpec((None, LANE, D_IDX), lambda b, j, pt: (b, 0, 0))]
    in_specs += [page_spec(g) for g in range(group)]
    return pl.pallas_call(
        kern,
        grid_spec=pltpu.PrefetchScalarGridSpec(
            num_scalar_prefetch=1,
            grid=(bsz, n_groups + 1),
            in_specs=in_specs,
            out_specs=pl.BlockSpec((None, group, ROWS, LANE), lambda b, j, pt: (b, j, 0, 0)),
        ),
        out_shape=jax.ShapeDtypeStruct((bsz, (n_groups + 1) * group, ROWS, LANE), F32),
        compiler_params=_cparams(("parallel", "arbitrary")),
        name="sample_indexer_scores",
    )(page_table, qi, wi, ki_new, *([cache_idx] * group))


def _sample_select_kernel(s_ref, o_ref, key_sc, *, topk):
    nch = s_ref.shape[0]

    def key_body(c, _):
        key_sc[c] = _sort_key(s_ref[c])
        return 0

    lax.fori_loop(0, nch, key_body, 0)

    unroll = 4 if nch % 4 == 0 else 1

    rows = s_ref.shape[1]

    def count(pred):
        def body(c, a):
            for u in range(unroll):
                a = a + jnp.where(pred(key_sc[c * unroll + u]), 1.0, 0.0)
            return a
        cnt = lax.fori_loop(0, nch // unroll, body, jnp.zeros((rows, LANE), F32))
        return jnp.sum(cnt, axis=1, keepdims=True)

    def count_ge(cand):
        candb = jnp.broadcast_to(cand, (rows, LANE))
        return count(lambda k: k >= candb)

    t = _kth_largest_key(count_ge, (rows, 1), topk)
    tb = jnp.broadcast_to(t, (rows, LANE))
    need = float(topk) - count(lambda k: k > tb)
    pmat = _prefix_matrix(LANE)

    def sel_body(c, carry):
        o_ref[c], carry = _select_mask(key_sc[c], t, need, carry, pmat)
        return carry

    lax.fori_loop(0, nch, sel_body, jnp.zeros((rows, 1), F32))


def sample_select_mask(scores, topk):
    bsz, nch = scores.shape[:2]
    rows = bsz * ROWS
    stacked = jnp.swapaxes(scores, 0, 1).reshape(nch, rows, LANE)
    blk = pl.BlockSpec((nch, rows, LANE), lambda i: (0, 0, 0))
    mask = pl.pallas_call(
        functools.partial(_sample_select_kernel, topk=topk),
        grid=(1,),
        in_specs=[blk],
        out_specs=blk,
        out_shape=jax.ShapeDtypeStruct((nch, rows, LANE), F32),
        scratch_shapes=[pltpu.VMEM((nch, rows, LANE), I32)],
        compiler_params=_cparams(("arbitrary",)),
        name="sample_select_mask",
    )(stacked)
    return jnp.swapaxes(mask.reshape(nch, bsz, ROWS, LANE), 0, 1)


def _layer_norm(x, g, b):
    mu = jnp.mean(x, axis=1, keepdims=True)
    cen = x - mu
    var = jnp.mean(cen * cen, axis=1, keepdims=True)
    return cen * lax.rsqrt(var + LN_EPS) * g + b


def _out_kernel(oa_ref, ob_ref, oc_ref, w_ref, x_ref, ga_ref, lng_ref, lnb_ref, sc_ref, sh_ref,
                wr_ref, br_ref, x1_ref, h_ref, lg_ref, *, alpha):
    attn = (_dot(oa_ref[...], w_ref[0:W_A, :])
            + _dot(ob_ref[...], w_ref[W_A:W_A + W_B, :])
            + _dot(oc_ref[...], w_ref[W_A + W_B:W_A + W_B + W_C, :]))
    x1 = _layer_norm(alpha * x_ref[...] + (1.0 + ga_ref[...]) * attn, lng_ref[...], lnb_ref[...])
    x1_ref[...] = x1
    h = x1 * (1.0 + sc_ref[...]) + sh_ref[...]
    h_ref[...] = h
    hh = h.astype(BF)
    hl = (h - hh.astype(F32)).astype(BF)
    w = wr_ref[...]
    wh = w.astype(BF)
    wl = (w - wh.astype(F32)).astype(BF)
    lg_ref[...] = _dot(hh, wh) + _dot(hh, wl) + _dot(hl, wh) + br_ref[...]


def out_projection(oa, ob, oc, w_out, x, g_a, ln_g, ln_b, sc_f, sh_f, w_router, b_router, layer, alpha, tm):
    b, t, d = x.shape
    ts = g_a.shape[1]
    tms = 1 if ts == 1 else tm
    mod_map = (lambda bi, i: (bi, 0, 0)) if ts == 1 else (lambda bi, i: (bi, i, 0))
    ne = w_router.shape[-1]
    row = lambda w: pl.BlockSpec((None, tm, w), lambda bi, i: (bi, i, 0))
    mod = pl.BlockSpec((None, tms, d), mod_map)
    par = lambda w: pl.BlockSpec((None, 1, w), lambda bi, i: (layer, 0, 0))
    return pl.pallas_call(
        functools.partial(_out_kernel, alpha=alpha),
        grid=(b, t // tm),
        in_specs=[row(W_A), row(W_B), row(W_C),
                  pl.BlockSpec((None, w_out.shape[1], d), lambda bi, i: (layer, 0, 0)),
                  row(d), mod, par(d), par(d), mod, mod,
                  pl.BlockSpec((None, d, ne), lambda bi, i: (layer, 0, 0)), par(ne)],
        out_specs=[row(d), row(d), row(ne)],
        out_shape=[jax.ShapeDtypeStruct((b, t, d), F32), jax.ShapeDtypeStruct((b, t, d), F32),
                   jax.ShapeDtypeStruct((b, t, ne), F32)],
        compiler_params=_cparams(("parallel", "parallel")),
        name="out_projection",
    )(oa, ob, oc, w_out, x, g_a, ln_g, ln_b, sc_f, sh_f, w_router, b_router)


def _router_kernel(lg_ref, idx_ref, w_ref):
    lg = lg_ref[...]
    n, ne = lg.shape
    lane = _iota((n, ne), 1).astype(F32)
    vals, idxs = [], []
    for _ in range(TOP_K_EXPERTS):
        mx = jnp.max(lg, axis=1, keepdims=True)
        ix = jnp.min(jnp.where(lg == mx, lane, float(ne)), axis=1, keepdims=True)
        vals.append(mx)
        idxs.append(ix)
        lg = jnp.where(lane == ix, -jnp.inf, lg)
    es = [jnp.exp(v - vals[0]) for v in vals]
    tot = es[0]
    for e in es[1:]:
        tot = tot + e
    out_lane = _iota((n, TOP_K_EXPERTS), 1)
    idx = jnp.zeros((n, TOP_K_EXPERTS), F32)
    wgt = jnp.zeros((n, TOP_K_EXPERTS), F32)
    for k in range(TOP_K_EXPERTS):
        idx = jnp.where(out_lane == k, idxs[k], idx)
        wgt = jnp.where(out_lane == k, es[k] / tot, wgt)
    idx_ref[...] = idx.astype(I32)
    w_ref[...] = wgt


def router_topk(logits, tm):
    n, ne = logits.shape
    return pl.pallas_call(
        _router_kernel,
        grid=(n // tm,),
        in_specs=[pl.BlockSpec((tm, ne), lambda i: (i, 0))],
        out_specs=[pl.BlockSpec((tm, TOP_K_EXPERTS), lambda i: (i, 0))] * 2,
        out_shape=[jax.ShapeDtypeStruct((n, TOP_K_EXPERTS), I32), jax.ShapeDtypeStruct((n, TOP_K_EXPERTS), F32)],
        compiler_params=_cparams(("parallel",)),
        name="router_topk",
    )(logits)


def _gate_up_kernel(te_ref, tv_ref, x_ref, wg_ref, wu_ref, bg_ref, bu_ref, o_ref, wg_sc, wu_sc):
    i = pl.program_id(1)
    changed = jnp.logical_or(i == 0, te_ref[i] != te_ref[jnp.maximum(i - 1, 0)])

    @pl.when(changed)
    def _():
        wg_sc[...] = wg_ref[...].astype(BF)
        wu_sc[...] = wu_ref[...].astype(BF)

    @pl.when(tv_ref[i] > 0)
    def _():
        x = x_ref[...].astype(BF)
        g = jnp.minimum(_dot(x, wg_sc[...]) + bg_ref[...], SWIGLU_LIMIT)
        u = jnp.clip(_dot(x, wu_sc[...]) + bu_ref[...], -SWIGLU_LIMIT, SWIGLU_LIMIT)
        o_ref[...] = ((u + 1.0) * g * jax.nn.sigmoid(SWIGLU_ALPHA * g)).astype(BF)

    @pl.when(tv_ref[i] == 0)
    def _():
        o_ref[...] = jnp.zeros(o_ref.shape, BF)


def expert_gate_up(tile_expert, tile_valid, xs, w_gate_up, b_gate_up, layer, tm, tn):
    r, d = xs.shape
    dff = w_gate_up.shape[-1] // 2
    nj = dff // tn
    ne = w_gate_up.shape[1]
    bias = b_gate_up.reshape(b_gate_up.shape[0], ne, 1, 2 * dff)
    return pl.pallas_call(
        _gate_up_kernel,
        grid_spec=pltpu.PrefetchScalarGridSpec(
            num_scalar_prefetch=2,
            grid=(nj, r // tm),
            in_specs=[
                pl.BlockSpec((tm, d), lambda j, i, te, tv: (i, 0)),
                pl.BlockSpec((None, None, d, tn), lambda j, i, te, tv: (layer, te[i], 0, j)),
                pl.BlockSpec((None, None, d, tn), lambda j, i, te, tv: (layer, te[i], 0, j + nj)),
                pl.BlockSpec((None, None, 1, tn), lambda j, i, te, tv: (layer, te[i], 0, j)),
                pl.BlockSpec((None, None, 1, tn), lambda j, i, te, tv: (layer, te[i], 0, j + nj)),
            ],
            out_specs=pl.BlockSpec((tm, tn), lambda j, i, te, tv: (i, j)),
            scratch_shapes=[pltpu.VMEM((d, tn), BF)] * 2,
        ),
        out_shape=jax.ShapeDtypeStruct((r, dff), BF),
        compiler_params=_cparams(("arbitrary", "arbitrary")),
        name="expert_gate_up",
    )(tile_expert, tile_valid, xs, w_gate_up, w_gate_up, bias, bias)


def _down_kernel(te_ref, tv_ref, a_ref, w_ref, b_ref, o_ref, w_sc):
    i = pl.program_id(1)
    changed = jnp.logical_or(i == 0, te_ref[i] != te_ref[jnp.maximum(i - 1, 0)])

    @pl.when(changed)
    def _():
        w_sc[...] = w_ref[...].astype(BF)

    @pl.when(tv_ref[i] > 0)
    def _():
        o_ref[...] = _dot(a_ref[...], w_sc[...]) + b_ref[...]

    @pl.when(tv_ref[i] == 0)
    def _():
        o_ref[...] = jnp.zeros(o_ref.shape, F32)


def expert_down(tile_expert, tile_valid, act, w_down, b_down, layer, tm, tn):
    r, dff = act.shape
    d = w_down.shape[-1]
    ne = w_down.shape[1]
    bias = b_down.reshape(b_down.shape[0], ne, 1, d)
    return pl.pallas_call(
        _down_kernel,
        grid_spec=pltpu.PrefetchScalarGridSpec(
            num_scalar_prefetch=2,
            grid=(d // tn, r // tm),
            in_specs=[
                pl.BlockSpec((tm, dff), lambda j, i, te, tv: (i, 0)),
                pl.BlockSpec((None, None, dff, tn), lambda j, i, te, tv: (layer, te[i], 0, j)),
                pl.BlockSpec((None, None, 1, tn), lambda j, i, te, tv: (layer, te[i], 0, j)),
            ],
            out_specs=pl.BlockSpec((tm, tn), lambda j, i, te, tv: (i, j)),
            scratch_shapes=[pltpu.VMEM((dff, tn), BF)],
        ),
        out_shape=jax.ShapeDtypeStruct((r, d), F32),
        compiler_params=_cparams(("arbitrary", "arbitrary")),
        name="expert_down",
    )(tile_expert, tile_valid, act, w_down, bias)


def _ffn_norm_kernel(x_ref, y_ref, w_ref, g_ref, lng_ref, lnb_ref, o_ref, *, alpha):
    w = w_ref[...]
    ffn = y_ref[0] * w[:, 0:1]
    for k in range(1, TOP_K_EXPERTS):
        ffn = ffn + y_ref[k] * w[:, k:k + 1]
    o_ref[...] = _layer_norm(alpha * x_ref[...] + (1.0 + g_ref[...]) * ffn, lng_ref[...], lnb_ref[...])


def ffn_combine_norm(x1, y4, top_w, g_f, ln_g, ln_b, layer, alpha, tm):
    b, t, d = x1.shape
    ts = g_f.shape[1]
    tms = 1 if ts == 1 else tm
    mod_map = (lambda bi, i: (bi, 0, 0)) if ts == 1 else (lambda bi, i: (bi, i, 0))
    row = pl.BlockSpec((None, tm, d), lambda bi, i: (bi, i, 0))
    par = pl.BlockSpec((None, 1, d), lambda bi, i: (layer, 0, 0))
    return pl.pallas_call(
        functools.partial(_ffn_norm_kernel, alpha=alpha),
        grid=(b, t // tm),
        in_specs=[row, pl.BlockSpec((TOP_K_EXPERTS, None, tm, d), lambda bi, i: (0, bi, i, 0)),
                  pl.BlockSpec((None, tm, TOP_K_EXPERTS), lambda bi, i: (bi, i, 0)),
                  pl.BlockSpec((None, tms, d), mod_map), par, par],
        out_specs=row,
        out_shape=jax.ShapeDtypeStruct((b, t, d), F32),
        compiler_params=_cparams(("parallel", "parallel")),
        name="ffn_combine_norm",
    )(x1, y4, top_w, g_f, ln_g, ln_b)


def _route(top_idx, n_experts, tm):
    n, k = top_idx.shape
    experts = jnp.arange(n_experts, dtype=I32)[None, :]
    hits = [(top_idx[:, s:s + 1] == experts) for s in range(k)]
    multi = sum(h.astype(I32) for h in hits)
    incl = jnp.cumsum(multi, axis=0)
    sizes = incl[-1]
    padded = ((sizes + tm - 1) // tm) * tm
    pend = jnp.cumsum(padded)
    base = incl - multi + (pend - padded)[None, :]
    n_rows = ((n * k + n_experts * (tm - 1)) // tm + 1) * tm
    dest = jnp.stack([jnp.sum(jnp.where(h, base, 0), axis=1) for h in hits], axis=1)
    src_token = jnp.zeros((n_rows,), I32).at[dest.reshape(-1)].set(jnp.arange(n * k, dtype=I32) // k)
    tile_start = jnp.arange(n_rows // tm, dtype=I32) * tm
    te = jnp.minimum(jnp.sum((pend[None, :] <= tile_start[:, None]).astype(I32), axis=1), n_experts - 1)
    tv = (tile_start < pend[-1]).astype(I32)
    return src_token, dest, te, tv


def moe_experts(h_tok, top_idx, w_gate_up, b_gate_up, w_down, b_down, layer, tm_route):
    d = h_tok.shape[1]
    ne = w_gate_up.shape[1]
    dff = w_down.shape[2]
    src_token, dest, te, tv = _route(top_idx, ne, tm_route)
    xs = jnp.take(h_tok, src_token, axis=0)
    act = expert_gate_up(te, tv, xs, w_gate_up, b_gate_up, layer, tm_route, min(1024, dff))
    return expert_down(te, tv, act, w_down, b_down, layer, tm_route, min(1024, d)), dest


def _pad_w_in(w_in):
    depth, d, _ = w_in.shape
    z = lambda n: jnp.zeros((depth, d, n), w_in.dtype)
    return jnp.concatenate([
        w_in[:, :, :S_WI], w_in[:, :, S_WI:S_KI], z(LANE - H_IDX),
        w_in[:, :, S_KI:S_QC], z(LANE - D_IDX), w_in[:, :, S_QC:]], axis=2).astype(BF)


def _pick(n, prefs):
    for p in prefs:
        if n % p == 0:
            return p
    return n


def kernel(x_prompt, x_sample, cache_kv_a, cache_kv_b, cache_idx_b, cache_kv_c, page_table, c_prompt, c_sample,
           w_ada, b_ada, w_in, ln_idx_g, ln_idx_b, lam_params, gn_a_gain, w_out, ln1_g, ln1_b, w_router,
           b_router, w_gate_up, b_gate_up, w_down, b_down, ln2_g, ln2_b):
    depth = w_in.shape[0]
    bp, tp, d = x_prompt.shape
    bs, t_new, _ = x_sample.shape
    n_pages = page_table.shape[1]
    past_len = n_pages * cache_kv_a.shape[2]
    alpha = (2 * depth) ** 0.25
    topk_p = min(TOPK_MAX, tp // 4)
    topk_s = min(TOPK_MAX, (past_len + t_new) // 4)
    ne = w_router.shape[-1]
    group = _pick(n_pages, (8, 4, 2, 1))

    w_in_p = _pad_w_in(w_in)
    w_out_b = w_out.astype(BF)
    pad_lane = lambda v: jnp.pad(v, ((0, 0), (0, LANE - v.shape[1])))
    ln_gb = jnp.stack([pad_lane(ln_idx_g), pad_lane(ln_idx_b)], axis=1)
    gain = gn_a_gain.reshape(depth, 1, W_A)
    vec = lambda v: v.reshape(depth, 1, -1)

    pos_p = jnp.arange(tp, dtype=I32)
    pos_s = past_len + jnp.arange(t_new, dtype=I32)
    tab64_p = jnp.tile(rope_tables(pos_p, DA_QK), (bp, 1))
    tab128_p = jnp.tile(rope_tables(pos_p, HEAD_DIM), (bp, 1))
    tab64_s = jnp.tile(rope_tables(pos_s, DA_QK), (bs, 1))
    tab128_s = jnp.tile(rope_tables(pos_s, HEAD_DIM), (bs, 1))

    n_c = bp + bs
    rows_c = -(-n_c // 8) * 8
    c_all = jnp.pad(jnp.concatenate([c_prompt, c_sample], axis=0), ((0, rows_c - n_c), (0, 0)))
    mod = ada_modulation(c_all, w_ada, b_ada, tn=_pick(6 * d, (1024, 512, 256, 128))).reshape(depth, rows_c, 6, d)

    tm_proj = _pick(tp, (1024, 512, 256, 128))
    tm_epi = _pick(tp, (256, 128))
    tq_a = _pick(tp, (512, 256, 128))
    tq_b = _pick(tp, (512, 256, 128))
    tm_out = _pick(tp, (256, 128))
    ms = bs * t_new
    kva_pages, kvb_pages, kvc_pages = [flat_page_view(c) for c in (cache_kv_a, cache_kv_b, cache_kv_c)]
    idx_pages = jnp.swapaxes(cache_idx_b, 2, 3)

    xp, xs = x_prompt, x_sample.reshape(1, ms, d)
    outs_p, outs_s = [], []
    first = jnp.arange(HEAD_DIM) < DA_QK
    for l in range(depth):
        lam_init = 0.8 - 0.6 * math.exp(-0.3 * l)
        mp = mod[l, :bp][:, :, None, :]
        msr = jnp.repeat(mod[l, bp:n_c], t_new, axis=0)[None]
        sh_a, sc_a, g_a, sh_f, sc_f, g_f = [mp[:, i] for i in range(6)]
        sh_as, sc_as, g_as, sh_fs, sc_fs, g_fs = [msr[:, :, i] for i in range(6)]

        proj = modulated_projection(xp, sc_a, sh_a, w_in_p, l, tm_proj).reshape(bp * tp, N_PAD)
        (kv_a, kv_b, idx, kv_c, qa, ka, va, qb, kb, vb, qi, wi, ki2, qc, kc, vc) = projection_epilogue(
            proj, tab64_p, tab128_p, ln_gb[l], tm_epi)
        r3 = lambda a: a.reshape(bp, tp, a.shape[-1])
        oa = prompt_attention_a(r3(qa), r3(ka), r3(va), lam_params, gain, l, lam_init, tq_a, tq_a)
        ob = prompt_attention_b(r3(qb), r3(qi), r3(wi), r3(kb), r3(vb), r3(ki2), topk_p, tq_b)
        oc = prompt_attention_c(r3(qc), r3(kc), r3(vc), tq_a, LANE)
        x1p, hp, lgp = out_projection(oa, ob, oc, w_out_b, xp, g_a, vec(ln1_g), vec(ln1_b), sc_f, sh_f,
                                      w_router, vec(b_router), l, alpha, tm_out)
        outs_p.append((kv_a.reshape(bp, tp, 2, H_A, HEAD_DIM), kv_b.reshape(bp, tp, 2, H_B, HEAD_DIM),
                       idx.reshape(bp, tp, D_IDX), kv_c.reshape(bp, tp, 2, H_C, HEAD_DIM)))

        proj_s = modulated_projection(xs, sc_as, sh_as, w_in_p, l, ms).reshape(ms, N_PAD)
        (kv_a_s, kv_b_s, idx_s, kv_c_s, qa_s, ka_s, va_s, qb_s, kb_s, vb_s, qi_s, wi_s, ki2_s, qc_s, kc_s,
         vc_s) = projection_epilogue(proj_s, tab64_s, tab128_s, ln_gb[l], ms)
        new_rows = lambda a: jnp.pad(a.reshape(bs, t_new, a.shape[-1]), ((0, 0), (0, LANE - t_new), (0, 0)))
        heads_first = lambda a, h: jnp.swapaxes(a.reshape(bs, t_new, h, HEAD_DIM), 1, 2)
        pad_rows = lambda a: jnp.pad(a, ((0, 0), (0, 0), (0, ROWS - t_new), (0, 0)))
        qa_h = heads_first(qa_s, H_A)
        qa_2 = jnp.concatenate([jnp.where(first, qa_h, 0), jnp.where(first, 0, qa_h)], axis=2).astype(BF)
        oa_s = sample_paged_softmax("a", page_table, kva_pages, l, qa_2, new_rows(ka_s), new_rows(va_s),
                                    lam_params=lam_params, gain=gain, lam_init=lam_init, t_new=t_new, group=group)
        qi_h = pad_rows(jnp.swapaxes(qi_s.reshape(bs, t_new, H_IDX, D_IDX), 1, 2)).reshape(bs, H_IDX * ROWS, D_IDX)
        wi_h = pad_rows(jnp.swapaxes(wi_s[:, :H_IDX].reshape(bs, t_new, H_IDX, 1), 1, 2)).reshape(
            bs, H_IDX * ROWS, 1)
        scores = sample_indexer_scores(page_table, idx_pages, l, qi_h, wi_h, new_rows(ki2_s[:, :D_IDX]),
                                       t_new=t_new, group=group)
        sel = sample_select_mask(scores, topk_s)
        ob_s = sample_paged_softmax("b", page_table, kvb_pages, l, pad_rows(heads_first(qb_s, H_B)),
                                    new_rows(kb_s), new_rows(vb_s), mask=sel, t_new=t_new, group=group)
        oc_s = sample_paged_stick(page_table, kvc_pages, l, pad_rows(heads_first(qc_s, H_C)),
                                  new_rows(kc_s), new_rows(vc_s), t_new=t_new, group=group)
        tok = lambda o, h: jnp.swapaxes(o[:, :, :t_new], 1, 2).reshape(1, ms, h * HEAD_DIM).astype(BF)
        x1s, hs, lgs = out_projection(tok(oa_s, H_A), tok(ob_s, H_B), tok(oc_s, H_C), w_out_b, xs, g_as,
                                      vec(ln1_g), vec(ln1_b), sc_fs, sh_fs, w_router, vec(b_router), l, alpha, ms)
        outs_s.append((kv_a_s.reshape(bs, t_new, 2, H_A, HEAD_DIM), kv_b_s.reshape(bs, t_new, 2, H_B, HEAD_DIM),
                       idx_s.reshape(bs, t_new, D_IDX), kv_c_s.reshape(bs, t_new, 2, H_C, HEAD_DIM)))

        np_ = bp * tp
        idx_p, w_p = router_topk(lgp.reshape(np_, ne), tm_out)
        idx_s, w_s = router_topk(lgs.reshape(ms, ne), ms)
        h_tok = jnp.concatenate([hp.reshape(np_, d), hs.reshape(ms, d)], axis=0)
        y, dest = moe_experts(h_tok, jnp.concatenate([idx_p, idx_s], axis=0), w_gate_up, b_gate_up,
                              w_down, b_down, l, 256)
        y4p = jnp.take(y, dest[:np_].T.reshape(TOP_K_EXPERTS, bp, tp), axis=0)
        y4s = jnp.take(y, dest[np_:].T.reshape(TOP_K_EXPERTS, 1, ms), axis=0)
        xp = ffn_combine_norm(x1p, y4p, w_p.reshape(bp, tp, TOP_K_EXPERTS), g_f, vec(ln2_g), vec(ln2_b),
                              l, alpha, tm_out)
        xs = ffn_combine_norm(x1s, y4s, w_s.reshape(1, ms, TOP_K_EXPERTS), g_fs, vec(ln2_g), vec(ln2_b),
                              l, alpha, ms)

    stack = lambda outs, i: jnp.stack([o[i] for o in outs])
    return (xp, xs.reshape(bs, t_new, d),
            stack(outs_p, 0), stack(outs_p, 1), stack(outs_p, 2), stack(outs_p, 3),
            stack(outs_s, 0), stack(outs_s, 1), stack(outs_s, 2), stack(outs_s, 3))
```
